```python
import math
import jax, jax.numpy as jnp
from jax import lax
import numpy as np

D_MODEL = 4096
BATCH = 8
SEQ = 2048
DEPTH = 4
DEC_BATCH = 16
DEC_SEQ = 32
PAST_LEN = 1024

CHUNK = 64
W_BRANCH = D_MODEL // 2
N_BRANCH = 3
POOL_WINDOWS = (2, 4, 8, 16)
POOL_GROUP = W_BRANCH // len(POOL_WINDOWS)
POOL_HIST = max(POOL_WINDOWS) - 1
HG_DK = 128
HG_HEADS = W_BRANCH // HG_DK
HG_DV = W_BRANCH // HG_HEADS
HG_BLOCK = 16
S5_GROUP = 16
S5_GROUPS = W_BRANCH // S5_GROUP
S5_STATE = 64
D_FF = 11008
CONV_W = 3
PLE_DIM = 256
N_IN = 6 * W_BRANCH + N_BRANCH * D_MODEL
EPS = 1e-6

kernel_name = 'hybrid_pool_hgrn2_s5_streaming_step'


def rms_norm(x, g):
    xf = x.astype(jnp.float32)
    y = xf * lax.rsqrt(jnp.mean(xf * xf, axis=-1, keepdims=True) + EPS)
    return (y * g.astype(jnp.float32)).astype(x.dtype)


def pool_mixer(u, hist, pos0, w_pool, scale):
    L = u.shape[1]
    full = jnp.concatenate([hist.astype(u.dtype), u], axis=1)
    uf = full.astype(jnp.float32)
    cs = jnp.cumsum(uf, axis=1)
    cs = jnp.concatenate([jnp.zeros_like(cs[:, :1]), cs], axis=1)
    pos = pos0 + jnp.arange(L)
    outs = []
    for gi, w in enumerate(POOL_WINDOWS):
        sl = slice(gi * POOL_GROUP, (gi + 1) * POOL_GROUP)
        win = cs[:, POOL_HIST + 1:POOL_HIST + 1 + L, sl] - cs[:, POOL_HIST + 1 - w:POOL_HIST + 1 - w + L, sl]
        cnt = jnp.minimum(pos + 1, w).astype(jnp.float32)[None, :, None]
        pooled = win / cnt - uf[:, POOL_HIST:, sl]
        outs.append(jnp.einsum('blc,cd->bld', pooled, w_pool[gi].astype(jnp.float32)))
    y = jnp.concatenate(outs, axis=-1) * scale.astype(jnp.float32)
    return y.astype(u.dtype), full[:, -POOL_HIST:]


def hgrn2_mixer(q, f_pre, i_v, g, lb, s0, norm_g):
    B, L, _ = q.shape
    f32 = jnp.float32
    lbf = lb.astype(f32)
    f = lbf + (1.0 - lbf) * jax.nn.sigmoid(f_pre.astype(f32))
    logf = jnp.log(f)
    k = 1.0 - f
    pad = (-L) % HG_BLOCK
    n_blk = (L + pad) // HG_BLOCK

    def blocks(a, d):
        a = a.astype(f32).reshape(B, L, HG_HEADS, d)
        a = jnp.pad(a, ((0, 0), (0, pad), (0, 0), (0, 0)))
        return a.reshape(B, n_blk, HG_BLOCK, HG_HEADS, d).transpose(1, 0, 2, 3, 4)

    xs = (blocks(q, HG_DK), blocks(k, HG_DK), blocks(i_v, HG_DV), blocks(logf, HG_DK))
    causal = jnp.tril(jnp.ones((HG_BLOCK, HG_BLOCK), dtype=bool))
    ref = HG_BLOCK // 2 - 1

    def step(S, blk):
        qb, kb, vb, lfb = blk
        b = jnp.cumsum(lfb, axis=1)
        b_ref = b[:, ref:ref + 1]
        b_last = b[:, -1]
        o_inter = jnp.einsum('bthk,bhkv->bthv', qb * jnp.exp(b), S)
        att = jnp.einsum('bthk,bshk->bhts', qb * jnp.exp(b - b_ref), kb * jnp.exp(b_ref - b))
        att = jnp.where(causal, att, 0.0)
        o_intra = jnp.einsum('bhts,bshv->bthv', att, vb)
        k_dec = kb * jnp.exp(b_last[:, None] - b)
        S_new = jnp.exp(b_last)[..., None] * S + jnp.einsum('bthk,bthv->bhkv', k_dec, vb)
        return S_new, o_inter + o_intra

    s_fin, o = lax.scan(step, s0.astype(f32), xs)
    o = o.transpose(1, 0, 2, 3, 4).reshape(B, n_blk * HG_BLOCK, HG_HEADS, HG_DV)[:, :L]
    o = o * lax.rsqrt(jnp.mean(o * o, axis=-1, keepdims=True) + EPS)
    o = o * norm_g.astype(f32).reshape(HG_HEADS, HG_DV)
    o = o.reshape(B, L, W_BRANCH) * jax.nn.silu(g.astype(f32))
    return o.astype(q.dtype), s_fin


def _s5_combine(e1, e2):
    a1r, a1i, b1r, b1i = e1
    a2r, a2i, b2r, b2i = e2
    return (a2r * a1r - a2i * a1i, a2r * a1i + a2i * a1r,
            a2r * b1r - a2i * b1i + b2r, a2r * b1i + a2i * b1r + b2i)


def s5_mixer(u, x0_re, x0_im, a_re, a_im, log_step, b_re, b_im, c_re, c_im, d_skip, w_glu, b_glu):
    Bn, L, _ = u.shape
    f32 = jnp.float32
    uf = u.astype(f32)
    ug = uf.reshape(Bn, L, S5_GROUPS, S5_GROUP)
    ar, ai = a_re.astype(f32), a_im.astype(f32)
    dt = jnp.exp(log_step.astype(f32))[:, None]
    mag = jnp.exp(dt * ar)
    ab_re = mag * jnp.cos(dt * ai)
    ab_im = mag * jnp.sin(dt * ai)
    den = ar * ar + ai * ai
    coef_re = ((ab_re - 1.0) * ar + ab_im * ai) / den
    coef_im = (ab_im * ar - (ab_re - 1.0) * ai) / den
    br, bi = b_re.astype(f32), b_im.astype(f32)
    bb_re = coef_re[..., None] * br - coef_im[..., None] * bi
    bb_im = coef_re[..., None] * bi + coef_im[..., None] * br
    bu_re = jnp.einsum('blgc,gpc->lbgp', ug, bb_re)
    bu_im = jnp.einsum('blgc,gpc->lbgp', ug, bb_im)
    x0r, x0i = x0_re.astype(f32), x0_im.astype(f32)
    bu_re = bu_re.at[0].add(ab_re * x0r - ab_im * x0i)
    bu_im = bu_im.at[0].add(ab_re * x0i + ab_im * x0r)
    a_t_re = jnp.broadcast_to(ab_re, (L, 1) + ab_re.shape)
    a_t_im = jnp.broadcast_to(ab_im, (L, 1) + ab_im.shape)
    _, _, xr, xi = lax.associative_scan(_s5_combine, (a_t_re, a_t_im, bu_re, bu_im), axis=0)
    y = (jnp.einsum('lbgp,gcp->blgc', xr, c_re.astype(f32))
         - jnp.einsum('lbgp,gcp->blgc', xi, c_im.astype(f32)))
    y = y.reshape(Bn, L, W_BRANCH) + d_skip.astype(f32) * uf
    y = jax.nn.gelu(y)
    y = y * jax.nn.sigmoid(y @ w_glu.astype(f32) + b_glu.astype(f32))
    return y.astype(u.dtype), xr[-1], xi[-1]


def conv_ffn(x, hist, w_up, conv_w, conv_b, w_down):
    L = x.shape[1]
    up = x @ w_up
    full = jnp.concatenate([hist.astype(up.dtype), up], axis=1)
    conv = conv_b + sum(conv_w[j] * full[:, j:j + L] for j in range(CONV_W))
    a, v = jnp.split(conv, 2, axis=-1)
    y = (jax.nn.gelu(a) * v) @ w_down
    return y, full[:, -(CONV_W - 1):]


def _layer_stack(x, p, st_pool, st_hg, st_s5r, st_s5i, st_conv, pos0, lb, prm):
    B, L, _ = x.shape
    W = W_BRANCH
    h = x
    n_pool, n_hg, n_s5r, n_s5i, n_conv = [], [], [], [], []
    for l in range(DEPTH):
        hn = rms_norm(h, prm['norm_mix'][l])
        proj = hn @ prm['w_in'][l]
        u_a = proj[..., 0:W]
        q = proj[..., W:2 * W]
        f_pre = proj[..., 2 * W:3 * W]
        i_v = proj[..., 3 * W:4 * W]
        g_b = proj[..., 4 * W:5 * W]
        u_c = proj[..., 5 * W:6 * W]
        gates = proj[..., 6 * W:].reshape(B, L, N_BRANCH, D_MODEL)
        o_a, s_pool = pool_mixer(u_a, st_pool[l], pos0, prm['pool_w'][l], prm['pool_scale'][l])
        o_b, s_hg = hgrn2_mixer(q, f_pre, i_v, g_b, lb[l], st_hg[l], prm['hg_norm'][l])
        o_c, s_r, s_i = s5_mixer(u_c, st_s5r[l], st_s5i[l], prm['s5_a_re'][l], prm['s5_a_im'][l],
                                 prm['s5_log_step'][l], prm['s5_b_re'][l], prm['s5_b_im'][l],
                                 prm['s5_c_re'][l], prm['s5_c_im'][l], prm['s5_d'][l],
                                 prm['s5_w_glu'][l], prm['s5_b_glu'][l])
        branches = jnp.stack([o_a, o_b, o_c], axis=2)
        z = jnp.einsum('blnw,nwd->blnd', branches, prm['w_br'][l])
        merged = jnp.sum(jax.nn.sigmoid(gates) * z, axis=2)
        h = h + merged @ prm['w_out'][l]
        f_out, s_conv = conv_ffn(rms_norm(h, prm['norm_ffn'][l]), st_conv[l], prm['w_up'][l],
                                 prm['conv_w'][l], prm['conv_b'][l], prm['w_down'][l])
        h = h + f_out
        ple_gate = jax.nn.sigmoid(rms_norm(h, prm['norm_ple'][l]) @ prm['w_ple_gate'][l])
        h = h + ple_gate * (p[l] @ prm['w_ple'][l])
        n_pool.append(s_pool)
        n_hg.append(s_hg)
        n_s5r.append(s_r)
        n_s5i.append(s_i)
        n_conv.append(s_conv)
    y = rms_norm(h, prm['norm_final'])
    return y, jnp.stack(n_pool), jnp.stack(n_hg), jnp.stack(n_s5r), jnp.stack(n_s5i), jnp.stack(n_conv)


def setup_inputs(seed: int = 0) -> dict:
    key = jax.random.key(seed)
    ks = iter(jax.random.split(key, 48))
    nrm = lambda shape, s: jax.random.normal(next(ks), shape, jnp.float32) * s
    gain = lambda shape: 1.0 + nrm(shape, 0.02)
    W = W_BRANCH
    u = jax.random.uniform(next(ks), (DEPTH, S5_GROUPS), jnp.float32)
    log_step = math.log(1e-3) + u * (math.log(1e-1) - math.log(1e-3))
    a_im = math.pi * jnp.arange(S5_STATE, dtype=jnp.float32)[None, None, :] + nrm((DEPTH, S5_GROUPS, S5_STATE), 0.01)
    return {
        'x_prompt': nrm((BATCH, SEQ, D_MODEL), 1.0),
        'x_sample': nrm((DEC_BATCH, DEC_SEQ, D_MODEL), 1.0),
        'state_pool': nrm((DEPTH, DEC_BATCH, POOL_HIST, W), 1.0),
        'state_hgrn': nrm((DEPTH, DEC_BATCH, HG_HEADS, HG_DK, HG_DV), 0.5),
        'state_s5_re': nrm((DEPTH, DEC_BATCH, S5_GROUPS, S5_STATE), 0.3),
        'state_s5_im': nrm((DEPTH, DEC_BATCH, S5_GROUPS, S5_STATE), 0.3),
        'state_ffn_conv': nrm((DEPTH, DEC_BATCH, CONV_W - 1, 2 * D_FF), 1.0),
        'p_prompt': nrm((DEPTH, BATCH, SEQ, PLE_DIM), 1.0),
        'p_sample': nrm((DEPTH, DEC_BATCH, DEC_SEQ, PLE_DIM), 1.0),
        'norm_mix': gain((DEPTH, D_MODEL)),
        'w_in': nrm((DEPTH, D_MODEL, N_IN), D_MODEL ** -0.5),
        'pool_w': nrm((DEPTH, len(POOL_WINDOWS), POOL_GROUP, POOL_GROUP), POOL_GROUP ** -0.5),
        'pool_scale': 1.0 + nrm((DEPTH, W), 0.1),
        'hg_lb_logits': nrm((DEPTH, W), 0.5),
        'hg_norm': gain((DEPTH, W)),
        's5_a_re': -0.5 + nrm((DEPTH, S5_GROUPS, S5_STATE), 0.01),
        's5_a_im': a_im,
        's5_log_step': log_step,
        's5_b_re': nrm((DEPTH, S5_GROUPS, S5_STATE, S5_GROUP), (2 * S5_GROUP) ** -0.5),
        's5_b_im': nrm((DEPTH, S5_GROUPS, S5_STATE, S5_GROUP), (2 * S5_GROUP) ** -0.5),
        's5_c_re': nrm((DEPTH, S5_GROUPS, S5_GROUP, S5_STATE), (2 * S5_STATE) ** -0.5),
        's5_c_im': nrm((DEPTH, S5_GROUPS, S5_GROUP, S5_STATE), (2 * S5_STATE) ** -0.5),
        's5_d': nrm((DEPTH, W), 1.0),
        's5_w_glu': nrm((DEPTH, W, W), W ** -0.5),
        's5_b_glu': nrm((DEPTH, W), 0.02),
        'w_br': nrm((DEPTH, N_BRANCH, W, D_MODEL), W ** -0.5),
        'w_out': nrm((DEPTH, D_MODEL, D_MODEL), D_MODEL ** -0.5),
        'norm_ffn': gain((DEPTH, D_MODEL)),
        'w_up': nrm((DEPTH, D_MODEL, 2 * D_FF), D_MODEL ** -0.5),
        'conv_w': nrm((DEPTH, CONV_W, 2 * D_FF), CONV_W ** -0.5),
        'conv_b': nrm((DEPTH, 2 * D_FF), 0.02),
        'w_down': nrm((DEPTH, D_FF, D_MODEL), D_FF ** -0.5),
        'norm_ple': gain((DEPTH, D_MODEL)),
        'w_ple_gate': nrm((DEPTH, D_MODEL, D_MODEL), D_MODEL ** -0.5),
        'w_ple': nrm((DEPTH, PLE_DIM, D_MODEL), PLE_DIM ** -0.5),
        'norm_final': gain((D_MODEL,)),
    }


def reference(x_prompt, x_sample, state_pool, state_hgrn, state_s5_re, state_s5_im, state_ffn_conv,
              p_prompt, p_sample, norm_mix, w_in, pool_w, pool_scale, hg_lb_logits, hg_norm,
              s5_a_re, s5_a_im, s5_log_step, s5_b_re, s5_b_im, s5_c_re, s5_c_im, s5_d, s5_w_glu, s5_b_glu,
              w_br, w_out, norm_ffn, w_up, conv_w, conv_b, w_down, norm_ple, w_ple_gate, w_ple, norm_final):
    prm = {'norm_mix': norm_mix, 'w_in': w_in, 'pool_w': pool_w, 'pool_scale': pool_scale,
           'hg_norm': hg_norm, 's5_a_re': s5_a_re, 's5_a_im': s5_a_im, 's5_log_step': s5_log_step,
           's5_b_re': s5_b_re, 's5_b_im': s5_b_im, 's5_c_re': s5_c_re, 's5_c_im': s5_c_im,
           's5_d': s5_d, 's5_w_glu': s5_w_glu, 's5_b_glu': s5_b_glu, 'w_br': w_br, 'w_out': w_out,
           'norm_ffn': norm_ffn, 'w_up': w_up, 'conv_w': conv_w, 'conv_b': conv_b, 'w_down': w_down,
           'norm_ple': norm_ple, 'w_ple_gate': w_ple_gate, 'w_ple': w_ple, 'norm_final': norm_final}
    lb_cum = jnp.cumsum(jax.nn.softmax(hg_lb_logits.astype(jnp.float32), axis=0), axis=0)
    lb = lb_cum - lb_cum[:1]
    bp = x_prompt.shape[0]
    z_pool = jnp.zeros((DEPTH, bp) + state_pool.shape[2:], x_prompt.dtype)
    z_hg = jnp.zeros((DEPTH, bp) + state_hgrn.shape[2:], jnp.float32)
    z_s5 = jnp.zeros((DEPTH, bp) + state_s5_re.shape[2:], jnp.float32)
    z_conv = jnp.zeros((DEPTH, bp) + state_ffn_conv.shape[2:], x_prompt.dtype)
    y_prompt, pool_p, hg_p, s5r_p, s5i_p, conv_p = _layer_stack(
        x_prompt, p_prompt, z_pool, z_hg, z_s5, z_s5, z_conv, 0, lb, prm)
    y_sample, pool_s, hg_s, s5r_s, s5i_s, conv_s = _layer_stack(
        x_sample, p_sample, state_pool, state_hgrn, state_s5_re, state_s5_im, state_ffn_conv, PAST_LEN, lb, prm)
    return (y_prompt, y_sample, pool_p, hg_p, s5r_p, s5i_p, conv_p, pool_s, hg_s, s5r_s, s5i_s, conv_s)
```

```python
import functools

import jax
import jax.numpy as jnp
from jax import lax
from jax.experimental import pallas as pl
from jax.experimental.pallas import tpu as pltpu

D_MODEL = 4096
DEPTH = 4
PAST_LEN = 1024
W_BRANCH = D_MODEL // 2
N_BRANCH = 3
POOL_WINDOWS = (2, 4, 8, 16)
POOL_GROUP = W_BRANCH // len(POOL_WINDOWS)
POOL_HIST = max(POOL_WINDOWS) - 1
HG_DK = 128
HG_HEADS = W_BRANCH // HG_DK
HG_DV = W_BRANCH // HG_HEADS
HG_BLOCK = 16
S5_GROUP = 16
S5_GROUPS = W_BRANCH // S5_GROUP
S5_STATE = 64
D_FF = 11008
CONV_W = 3
PLE_DIM = 256
N_MIX = 6 * W_BRANCH
N_IN = N_MIX + N_BRANCH * D_MODEL
EPS = 1e-6

F32 = jnp.float32
BF16 = jnp.bfloat16

VMEM_LIMIT_BYTES = 56 * 1024 * 1024
SUBLANES = 8
BF16_ROWS = 16

S5_CHUNK_GROUPS = 16
S5_CHUNKS = S5_GROUPS // S5_CHUNK_GROUPS
S5_CH = S5_CHUNK_GROUPS * S5_GROUP
S5_ST = S5_CHUNK_GROUPS * S5_STATE


def _params(*sem):
    return pltpu.CompilerParams(dimension_semantics=sem, vmem_limit_bytes=VMEM_LIMIT_BYTES)


def _dot(a, b):
    return jnp.dot(a, b, preferred_element_type=F32)


def _norm_kernel(x_ref, g_ref, o_ref):
    x = x_ref[...]
    ms = jnp.mean(x * x, axis=-1, keepdims=True)
    o_ref[...] = ((x * lax.rsqrt(ms + EPS)) * g_ref[...]).astype(o_ref.dtype)


def rms_norm(x, g, out_dtype):
    T, D = x.shape
    tr = min(T, 512)
    return pl.pallas_call(
        _norm_kernel,
        grid=(T // tr,),
        in_specs=[pl.BlockSpec((tr, D), lambda i: (i, 0)),
                  pl.BlockSpec((1, D), lambda i: (0, 0))],
        out_specs=pl.BlockSpec((tr, D), lambda i: (i, 0)),
        out_shape=jax.ShapeDtypeStruct((T, D), out_dtype),
        compiler_params=_params("parallel"),
        name="rms_norm",
    )(x, g.reshape(1, D))


def _mm_kernel(a_ref, b_ref, o_ref):
    o_ref[...] = _dot(a_ref[...], b_ref[...]).astype(o_ref.dtype)


def matmul(a, b, out_dtype, *, tm, tn, col_block0, n_cols, name):
    M, K = a.shape
    return pl.pallas_call(
        _mm_kernel,
        grid=(M // tm, n_cols // tn),
        in_specs=[pl.BlockSpec((tm, K), lambda i, j: (i, 0)),
                  pl.BlockSpec((K, tn), lambda i, j: (0, j + col_block0))],
        out_specs=pl.BlockSpec((tm, tn), lambda i, j: (i, j)),
        out_shape=jax.ShapeDtypeStruct((M, n_cols), out_dtype),
        compiler_params=_params("parallel", "arbitrary"),
        name=name,
    )(a, b)


def _mm_res_kernel(a_ref, b_ref, h_ref, o_ref):
    o_ref[...] = h_ref[...] + _dot(a_ref[...], b_ref[...])


def matmul_residual(a, b, h, *, tm, tn, name):
    M, K = a.shape
    N = b.shape[1]
    return pl.pallas_call(
        _mm_res_kernel,
        grid=(M // tm, N // tn),
        in_specs=[pl.BlockSpec((tm, K), lambda i, j: (i, 0)),
                  pl.BlockSpec((K, tn), lambda i, j: (0, j)),
                  pl.BlockSpec((tm, tn), lambda i, j: (i, j))],
        out_specs=pl.BlockSpec((tm, tn), lambda i, j: (i, j)),
        out_shape=jax.ShapeDtypeStruct((M, N), F32),
        input_output_aliases={2: 0},
        compiler_params=_params("parallel", "arbitrary"),
        name=name,
    )(a, b, h)


def _lb_kernel(x_ref, o_ref):
    x = x_ref[...]
    m = jnp.max(x, axis=0, keepdims=True)
    e = jnp.exp(x - m)
    p = e / jnp.sum(e, axis=0, keepdims=True)
    rows = [p[0:1]]
    for l in range(1, DEPTH):
        rows.append(rows[-1] + p[l:l + 1])
    o_ref[...] = jnp.concatenate([r - rows[0] for r in rows], axis=0)


def forget_lower_bounds(logits):
    return pl.pallas_call(
        _lb_kernel,
        out_shape=jax.ShapeDtypeStruct(logits.shape, F32),
        name="hgrn_lower_bounds",
    )(logits)


def _pool_kernel(u_ref, prev_ref, hist_ref, w_ref, scale_ref, o_ref, st_ref, ext_ref, *, tt, pos0):
    ti = pl.program_id(1)
    nt = pl.num_programs(1)
    hist_rows = POOL_HIST + 1

    @pl.when(ti == 0)
    def _():
        ext_ref[0:hist_rows, :] = hist_ref[0]

    @pl.when(ti > 0)
    def _():
        ext_ref[0:hist_rows, :] = prev_ref[...]

    cur = u_ref[...]
    ext_ref[hist_rows:, :] = cur
    pos = pos0 + ti * tt + lax.broadcasted_iota(jnp.int32, (tt, 1), 0)
    for gi, w in enumerate(POOL_WINDOWS):
        cols = slice(gi * POOL_GROUP, (gi + 1) * POOL_GROUP)
        win = ext_ref[pl.ds(hist_rows, tt), cols]
        for k in range(1, w):
            win = win + ext_ref[pl.ds(hist_rows - k, tt), cols]
        cnt = jnp.minimum(pos + 1, w).astype(F32)
        pooled = win / cnt - cur[:, cols]
        y = _dot(pooled.astype(BF16), w_ref[gi]) * scale_ref[:, cols]
        o_ref[:, cols] = y.astype(o_ref.dtype)

    @pl.when(ti == nt - 1)
    def _():
        st_ref[0] = ext_ref[pl.ds(tt + 1, POOL_HIST), :]


def pool_mixer(proj, hist, w_bf16, scale, *, B, L, pos0):
    W = W_BRANCH
    tt = min(L, 256)
    nt = L // tt
    hist16 = jnp.concatenate([jnp.zeros((B, 1, W), F32), hist], axis=1)
    blk16 = tt // (POOL_HIST + 1)
    kern = functools.partial(_pool_kernel, tt=tt, pos0=pos0)
    return pl.pallas_call(
        kern,
        grid=(B, nt),
        in_specs=[pl.BlockSpec((tt, W), lambda b, t: (b * nt + t, 0)),
                  pl.BlockSpec((POOL_HIST + 1, W),
                               lambda b, t: (jnp.maximum((b * nt + t) * blk16 - 1, 0), 0)),
                  pl.BlockSpec((1, POOL_HIST + 1, W), lambda b, t: (b, 0, 0)),
                  pl.BlockSpec((len(POOL_WINDOWS), POOL_GROUP, POOL_GROUP), lambda b, t: (0, 0, 0)),
                  pl.BlockSpec((1, W), lambda b, t: (0, 0))],
        out_specs=[pl.BlockSpec((tt, W), lambda b, t: (b * nt + t, 0)),
                   pl.BlockSpec((1, POOL_HIST, W), lambda b, t: (b, 0, 0))],
        out_shape=[jax.ShapeDtypeStruct((B * L, W), BF16),
                   jax.ShapeDtypeStruct((B, POOL_HIST, W), F32)],
        scratch_shapes=[pltpu.VMEM((tt + POOL_HIST + 1, W), F32)],
        compiler_params=_params("parallel", "arbitrary"),
        name="pool_mixer",
    )(proj, proj, hist16, w_bf16, scale.reshape(1, W))


HG_HEADS_PER_STEP = 4


def _hgrn_kernel(q_ref, f_ref, v_ref, g_ref, lb_ref, ng_ref, s0_ref, o_ref, s_ref, st_ref, oacc_ref, *, C):
    ci = pl.program_id(2)
    nc = pl.num_programs(2)
    hb = HG_HEADS_PER_STEP
    nsub = C // HG_BLOCK
    ref_row = HG_BLOCK // 2 - 1

    @pl.when(ci == 0)
    def _():
        for h in range(hb):
            st_ref[h] = s0_ref[0, h].T

    lb = lb_ref[...]
    f = lb + (1.0 - lb) * jax.nn.sigmoid(f_ref[...])
    logf = jnp.log(f)
    kk = 1.0 - f
    tri = (lax.broadcasted_iota(jnp.int32, (C, C), 1)
           <= lax.broadcasted_iota(jnp.int32, (C, C), 0)).astype(F32)
    bcum = jnp.dot(tri, logf, precision=lax.Precision.HIGHEST, preferred_element_type=F32)
    blast = bcum[C - 1:C, :]
    q = q_ref[...]
    v = v_ref[...]
    qd = (q * jnp.exp(bcum)).astype(BF16)
    kdec = (kk * jnp.exp(blast - bcum)).astype(BF16)
    eb = jnp.exp(blast)
    vb = v.astype(BF16)
    gate = jax.nn.silu(g_ref[...])
    ng = ng_ref[...]
    for h in range(hb):
        sl = slice(h * HG_DK, (h + 1) * HG_DK)
        st = st_ref[h]
        oacc_ref[...] = lax.dot_general(qd[:, sl], st.astype(BF16), (((1,), (1,)), ((), ())),
                                        preferred_element_type=F32)
        for j in range(nsub):
            r0 = j * HG_BLOCK
            bj = bcum[r0:, sl]
            bref = bcum[r0 + ref_row:r0 + ref_row + 1, sl]
            qj = (q[r0:, sl] * jnp.exp(bj - bref)).astype(BF16)
            kj = (kk[r0:r0 + HG_BLOCK, sl] * jnp.exp(bref - bj[:HG_BLOCK])).astype(BF16)
            att = lax.dot_general(qj, kj, (((1,), (1,)), ((), ())), preferred_element_type=F32)
            rr = lax.broadcasted_iota(jnp.int32, att.shape, 0)
            cc = lax.broadcasted_iota(jnp.int32, att.shape, 1)
            att = jnp.where(cc <= rr, att, 0.0)
            oacc_ref[r0:, :] += _dot(att.astype(BF16), vb[r0:r0 + HG_BLOCK, sl])
        o = oacc_ref[...]
        o = o * lax.rsqrt(jnp.mean(o * o, axis=-1, keepdims=True) + EPS)
        o = o * ng[:, sl]
        o_ref[:, sl] = (o * gate[:, sl]).astype(o_ref.dtype)
        upd = lax.dot_general(vb[:, sl], kdec[:, sl], (((0,), (0,)), ((), ())),
                              preferred_element_type=F32)
        st_ref[h] = eb[:, sl] * st + upd

    @pl.when(ci == nc - 1)
    def _():
        for h in range(hb):
            s_ref[0, h] = st_ref[h].T


def hgrn2_mixer(proj, lb, norm_g, s0, *, B, L):
    W = W_BRANCH
    C = min(L, 128)
    nc = L // C
    hb = HG_HEADS_PER_STEP
    nh = HG_HEADS // hb
    wb = hb * HG_DK
    per = W // wb

    def col(k):
        return pl.BlockSpec((C, wb), lambda b, h, c: (b * nc + c, k * per + h))

    vec = pl.BlockSpec((1, wb), lambda b, h, c: (0, h))
    st = pl.BlockSpec((1, hb, HG_DK, HG_DV), lambda b, h, c: (b, h, 0, 0))
    kern = functools.partial(_hgrn_kernel, C=C)
    return pl.pallas_call(
        kern,
        grid=(B, nh, nc),
        in_specs=[col(1), col(2), col(3), col(4), vec, vec, st],
        out_specs=[pl.BlockSpec((C, wb), lambda b, h, c: (b * nc + c, h)), st],
        out_shape=[jax.ShapeDtypeStruct((B * L, W), BF16),
                   jax.ShapeDtypeStruct((B, HG_HEADS, HG_DK, HG_DV), F32)],
        scratch_shapes=[pltpu.VMEM((hb, HG_DV, HG_DK), F32),
                        pltpu.VMEM((C, HG_DV), F32)],
        compiler_params=_params("parallel", "parallel", "arbitrary"),
        name="hgrn2_mixer",
    )(proj, proj, proj, proj, lb.reshape(1, W), norm_g.reshape(1, W), s0)


def _s5_kernel(u_ref, bw_ref, cw_ref, pw_ref, x0_ref, d_ref, y_ref, xo_ref, xr_s, xi_s, car_s, *, tt):
    ti = pl.program_id(2)
    nt = pl.num_programs(2)

    @pl.when(ti == 0)
    def _():
        x0 = x0_ref[0, 0]
        car_s[0] = jnp.broadcast_to(x0[0:1], (SUBLANES, S5_ST))
        car_s[1] = jnp.broadcast_to(x0[1:2], (SUBLANES, S5_ST))

    u = u_ref[...]
    bu = _dot(u.astype(BF16), bw_ref[0])
    xr_s[...] = bu[:, :S5_ST]
    xi_s[...] = bu[:, S5_ST:]

    def body(r, carry):
        cr, cim = carry
        off = pl.multiple_of(r * SUBLANES, SUBLANES)
        xr = xr_s[pl.ds(off, SUBLANES), :]
        xi = xi_s[pl.ds(off, SUBLANES), :]
        for idx, k in enumerate((1, 2, 4)):
            sr = pltpu.roll(xr, k, 0)
            si = pltpu.roll(xi, k, 0)
            mr = pw_ref[0, 2 * idx]
            mi = pw_ref[0, 2 * idx + 1]
            xr, xi = xr + (mr * sr - mi * si), xi + (mr * si + mi * sr)
        pr = pw_ref[0, 6]
        pim = pw_ref[0, 7]
        xr, xi = xr + (pr * cr - pim * cim), xi + (pr * cim + pim * cr)
        xr_s[pl.ds(off, SUBLANES), :] = xr
        xi_s[pl.ds(off, SUBLANES), :] = xi
        return (jnp.broadcast_to(xr[SUBLANES - 1:SUBLANES], (SUBLANES, S5_ST)),
                jnp.broadcast_to(xi[SUBLANES - 1:SUBLANES], (SUBLANES, S5_ST)))

    cr, cim = lax.fori_loop(0, tt // SUBLANES, body, (car_s[0], car_s[1]))
    car_s[0] = cr
    car_s[1] = cim

    y = _dot(xr_s[...].astype(BF16), cw_ref[0, 0]) + _dot(xi_s[...].astype(BF16), cw_ref[0, 1])
    y = y + d_ref[...] * u
    y_ref[...] = jax.nn.gelu(y).astype(y_ref.dtype)

    @pl.when(ti == nt - 1)
    def _():
        xo_ref[0, 0] = jnp.concatenate([cr[0:1], cim[0:1]], axis=0)


def _s5_discretize(a_re, a_im, log_step, b_re, b_im, c_re, c_im):
    dt = jnp.exp(log_step)[:, None]
    mag = jnp.exp(dt * a_re)
    ab_re = mag * jnp.cos(dt * a_im)
    ab_im = mag * jnp.sin(dt * a_im)
    den = a_re * a_re + a_im * a_im
    coef_re = ((ab_re - 1.0) * a_re + ab_im * a_im) / den
    coef_im = (ab_im * a_re - (ab_re - 1.0) * a_im) / den
    bb_re = coef_re[..., None] * b_re - coef_im[..., None] * b_im
    bb_im = coef_re[..., None] * b_im + coef_im[..., None] * b_re
    eye = jnp.eye(S5_CHUNK_GROUPS, dtype=F32)

    def blockdiag_b(bb):
        t = bb.reshape(S5_CHUNKS, S5_CHUNK_GROUPS, S5_STATE, S5_GROUP).transpose(0, 1, 3, 2)
        return jnp.einsum('cgxp,gh->cgxhp', t, eye).reshape(S5_CHUNKS, S5_CH, S5_ST)

    def blockdiag_c(cc):
        t = cc.reshape(S5_CHUNKS, S5_CHUNK_GROUPS, S5_GROUP, S5_STATE)
        return jnp.einsum('cgxp,gh->cgphx', t, eye).reshape(S5_CHUNKS, S5_ST, S5_CH)

    bw = jnp.concatenate([blockdiag_b(bb_re), blockdiag_b(bb_im)], axis=2).astype(BF16)
    cw = jnp.stack([blockdiag_c(c_re), -blockdiag_c(c_im)], axis=1).astype(BF16)

    pows = [(ab_re, ab_im)]
    for _ in range(SUBLANES - 1):
        pr, pi = pows[-1]
        pows.append((pr * ab_re - pi * ab_im, pr * ab_im + pi * ab_re))

    def plane(vals):
        return jnp.stack([v.reshape(S5_CHUNKS, S5_ST) for v in vals], axis=1)

    zero = jnp.zeros_like(ab_re)
    planes = []
    for k in (1, 2, 4):
        planes.append(plane([pows[k - 1][0] if t >= k else zero for t in range(SUBLANES)]))
        planes.append(plane([pows[k - 1][1] if t >= k else zero for t in range(SUBLANES)]))
    planes.append(plane([pows[t][0] for t in range(SUBLANES)]))
    planes.append(plane([pows[t][1] for t in range(SUBLANES)]))
    pw = jnp.stack(planes, axis=1)
    return bw, cw, pw


def s5_mixer(proj, x0_re, x0_im, disc, d_skip, *, B, L):
    W = W_BRANCH
    bw, cw, pw = disc
    tt = min(L, 512)
    nt = L // tt
    col0 = 5 * W // S5_CH
    x0 = jnp.stack([x0_re.reshape(B, S5_CHUNKS, S5_ST), x0_im.reshape(B, S5_CHUNKS, S5_ST)], axis=2)
    kern = functools.partial(_s5_kernel, tt=tt)
    y, xo = pl.pallas_call(
        kern,
        grid=(B, S5_CHUNKS, nt),
        in_specs=[pl.BlockSpec((tt, S5_CH), lambda b, c, t: (b * nt + t, col0 + c)),
                  pl.BlockSpec((1, S5_CH, 2 * S5_ST), lambda b, c, t: (c, 0, 0)),
                  pl.BlockSpec((1, 2, S5_ST, S5_CH), lambda b, c, t: (c, 0, 0, 0)),
                  pl.BlockSpec((1, 8, SUBLANES, S5_ST), lambda b, c, t: (c, 0, 0, 0)),
                  pl.BlockSpec((1, 1, 2, S5_ST), lambda b, c, t: (b, c, 0, 0)),
                  pl.BlockSpec((1, S5_CH), lambda b, c, t: (0, c))],
        out_specs=[pl.BlockSpec((tt, S5_CH), lambda b, c, t: (b * nt + t, c)),
                   pl.BlockSpec((1, 1, 2, S5_ST), lambda b, c, t: (b, c, 0, 0))],
        out_shape=[jax.ShapeDtypeStruct((B * L, W), BF16),
                   jax.ShapeDtypeStruct((B, S5_CHUNKS, 2, S5_ST), F32)],
        scratch_shapes=[pltpu.VMEM((tt, S5_ST), F32),
                        pltpu.VMEM((tt, S5_ST), F32),
                        pltpu.VMEM((2, SUBLANES, S5_ST), F32)],
        compiler_params=_params("parallel", "parallel", "arbitrary"),
        name="s5_scan",
    )(proj, bw, cw, pw, x0, d_skip.reshape(1, W))
    s_re = xo[:, :, 0, :].reshape(B, S5_GROUPS, S5_STATE)
    s_im = xo[:, :, 1, :].reshape(B, S5_GROUPS, S5_STATE)
    return y, s_re, s_im


def _glu_kernel(a_ref, b_ref, y_ref, bias_ref, o_ref):
    z = _dot(a_ref[...], b_ref[...]) + bias_ref[...]
    o_ref[...] = (y_ref[...].astype(F32) * jax.nn.sigmoid(z)).astype(o_ref.dtype)


def s5_glu(y, w_bf16, bias, *, tm, tn):
    M, K = y.shape
    N = w_bf16.shape[1]
    return pl.pallas_call(
        _glu_kernel,
        grid=(M // tm, N // tn),
        in_specs=[pl.BlockSpec((tm, K), lambda i, j: (i, 0)),
                  pl.BlockSpec((K, tn), lambda i, j: (0, j)),
                  pl.BlockSpec((tm, tn), lambda i, j: (i, j)),
                  pl.BlockSpec((1, tn), lambda i, j: (0, j))],
        out_specs=pl.BlockSpec((tm, tn), lambda i, j: (i, j)),
        out_shape=jax.ShapeDtypeStruct((M, N), BF16),
        compiler_params=_params("parallel", "arbitrary"),
        name="s5_glu",
    )(y, w_bf16, y, bias.reshape(1, N))


def _merge_kernel(oa_ref, ob_ref, oc_ref, w_ref, ga_ref, gb_ref, gc_ref, o_ref):
    acc = jax.nn.sigmoid(ga_ref[...].astype(F32)) * _dot(oa_ref[...], w_ref[0])
    acc = acc + jax.nn.sigmoid(gb_ref[...].astype(F32)) * _dot(ob_ref[...], w_ref[1])
    acc = acc + jax.nn.sigmoid(gc_ref[...].astype(F32)) * _dot(oc_ref[...], w_ref[2])
    o_ref[...] = acc.astype(o_ref.dtype)


def branch_merge(o_a, o_b, o_c, w_br_bf16, gates, *, tm, tn):
    M, W = o_a.shape
    D = D_MODEL
    nj = D // tn
    act = pl.BlockSpec((tm, W), lambda i, j: (i, 0))

    def gate(n):
        return pl.BlockSpec((tm, tn), lambda i, j: (i, n * nj + j))

    return pl.pallas_call(
        _merge_kernel,
        grid=(M // tm, nj),
        in_specs=[act, act, act,
                  pl.BlockSpec((N_BRANCH, W, tn), lambda i, j: (0, 0, j)),
                  gate(0), gate(1), gate(2)],
        out_specs=pl.BlockSpec((tm, tn), lambda i, j: (i, j)),
        out_shape=jax.ShapeDtypeStruct((M, D), BF16),
        compiler_params=_params("parallel", "arbitrary"),
        name="branch_merge",
    )(o_a, o_b, o_c, w_br_bf16, gates, gates, gates)


def _conv_gate(ua0, ua1, ua2, uv0, uv1, uv2, cwa_ref, cwv_ref, cba_ref, cbv_ref):
    a = cba_ref[...] + (cwa_ref[0:1, :] * ua0 + cwa_ref[1:2, :] * ua1 + cwa_ref[2:3, :] * ua2)
    v = cbv_ref[...] + (cwv_ref[0:1, :] * uv0 + cwv_ref[1:2, :] * uv1 + cwv_ref[2:3, :] * uv2)
    return jax.nn.gelu(a) * v


def _up_conv_long_kernel(x_ref, xp_ref, wa_ref, wv_ref, cwa_ref, cwv_ref, cba_ref, cbv_ref, hist_ref,
                         g_ref, tail_ref, xe_s, ua_s, uv_s, *, tm, tiles_per_seq):
    i = pl.program_id(0)
    j = pl.program_id(1)
    pad = BF16_ROWS

    @pl.when(j == 0)
    def _():
        xe_s[0:pad, :] = xp_ref[...]
        xe_s[pad:, :] = x_ref[...]

    xe = xe_s[...]
    ua_s[...] = _dot(xe, wa_ref[...])
    uv_s[...] = _dot(xe, wv_ref[...])

    @pl.when(i % tiles_per_seq == 0)
    def _():
        ua_s[pad - 2:pad, :] = hist_ref[0, 0]
        uv_s[pad - 2:pad, :] = hist_ref[1, 0]

    g = _conv_gate(ua_s[pl.ds(pad - 2, tm), :], ua_s[pl.ds(pad - 1, tm), :], ua_s[pl.ds(pad, tm), :],
                   uv_s[pl.ds(pad - 2, tm), :], uv_s[pl.ds(pad - 1, tm), :], uv_s[pl.ds(pad, tm), :],
                   cwa_ref, cwv_ref, cba_ref, cbv_ref)
    g_ref[...] = g.astype(g_ref.dtype)
    tail_ref[0, 0] = ua_s[pl.ds(pad + tm - 2, 2), :]
    tail_ref[1, 0] = uv_s[pl.ds(pad + tm - 2, 2), :]


def _up_conv_short_kernel(x_ref, wa_ref, wv_ref, cwa_ref, cwv_ref, cba_ref, cbv_ref, hist_ref,
                          g_ref, tail_ref, ua_s, uv_s, *, nseq, L):
    x = x_ref[...]
    tn = wa_ref.shape[1]
    pad = SUBLANES
    ua_s[:, pad:, :] = _dot(x, wa_ref[...]).reshape(nseq, L, tn)
    uv_s[:, pad:, :] = _dot(x, wv_ref[...]).reshape(nseq, L, tn)
    ua_s[:, pad - 2:pad, :] = hist_ref[0]
    uv_s[:, pad - 2:pad, :] = hist_ref[1]
    g = _conv_gate(ua_s[:, pad - 2:pad - 2 + L, :], ua_s[:, pad - 1:pad - 1 + L, :], ua_s[:, pad:pad + L, :],
                   uv_s[:, pad - 2:pad - 2 + L, :], uv_s[:, pad - 1:pad - 1 + L, :], uv_s[:, pad:pad + L, :],
                   cwa_ref, cwv_ref, cba_ref, cbv_ref)
    g_ref[...] = g.reshape(nseq * L, tn).astype(g_ref.dtype)
    tail_ref[0] = ua_s[:, pad + L - 2:pad + L, :]
    tail_ref[1] = uv_s[:, pad + L - 2:pad + L, :]


def conv_ffn_up(xn, w_a, w_v, conv_w, conv_b, hist, *, B, L):
    T, D = xn.shape
    tn = 512
    nj = pl.cdiv(D_FF, tn)
    cwa, cwv = conv_w[:, :D_FF], conv_w[:, D_FF:]
    cba, cbv = conv_b[:D_FF].reshape(1, D_FF), conv_b[D_FF:].reshape(1, D_FF)
    hist2 = jnp.stack([hist[:, :, :D_FF], hist[:, :, D_FF:]], axis=0)
    wspec = pl.BlockSpec((D, tn), lambda i, j: (0, j))
    cwspec = pl.BlockSpec((CONV_W, tn), lambda i, j: (0, j))
    cbspec = pl.BlockSpec((1, tn), lambda i, j: (0, j))
    if L >= 1024:
        tm = 1024
        tps = L // tm
        nti = T // tm
        kern = functools.partial(_up_conv_long_kernel, tm=tm, tiles_per_seq=tps)
        g, tails = pl.pallas_call(
            kern,
            grid=(nti, nj),
            in_specs=[pl.BlockSpec((tm, D), lambda i, j: (i, 0)),
                      pl.BlockSpec((BF16_ROWS, D),
                                   lambda i, j: (jnp.maximum(i * (tm // BF16_ROWS) - 1, 0), 0)),
                      wspec, wspec, cwspec, cwspec, cbspec, cbspec,
                      pl.BlockSpec((2, 1, 2, tn), lambda i, j: (0, i // tps, 0, j))],
            out_specs=[pl.BlockSpec((tm, tn), lambda i, j: (i, j)),
                       pl.BlockSpec((2, 1, 2, tn), lambda i, j: (0, i, 0, j))],
            out_shape=[jax.ShapeDtypeStruct((T, D_FF), BF16),
                       jax.ShapeDtypeStruct((2, nti, 2, D_FF), F32)],
            scratch_shapes=[pltpu.VMEM((tm + BF16_ROWS, D), BF16),
                            pltpu.VMEM((tm + BF16_ROWS, tn), F32),
                            pltpu.VMEM((tm + BF16_ROWS, tn), F32)],
            compiler_params=_params("parallel", "arbitrary"),
            name="conv_ffn_up_long",
        )(xn, xn, w_a, w_v, cwa, cwv, cba, cbv, hist2)
        tails = tails[:, tps - 1::tps]
    else:
        kern = functools.partial(_up_conv_short_kernel, nseq=B, L=L)
        g, tails = pl.pallas_call(
            kern,
            grid=(nj,),
            in_specs=[pl.BlockSpec((T, D), lambda j: (0, 0)),
                      pl.BlockSpec((D, tn), lambda j: (0, j)),
                      pl.BlockSpec((D, tn), lambda j: (0, j)),
                      pl.BlockSpec((CONV_W, tn), lambda j: (0, j)),
                      pl.BlockSpec((CONV_W, tn), lambda j: (0, j)),
                      pl.BlockSpec((1, tn), lambda j: (0, j)),
                      pl.BlockSpec((1, tn), lambda j: (0, j)),
                      pl.BlockSpec((2, B, 2, tn), lambda j: (0, 0, 0, j))],
            out_specs=[pl.BlockSpec((T, tn), lambda j: (0, j)),
                       pl.BlockSpec((2, B, 2, tn), lambda j: (0, 0, 0, j))],
            out_shape=[jax.ShapeDtypeStruct((T, D_FF), BF16),
                       jax.ShapeDtypeStruct((2, B, 2, D_FF), F32)],
            scratch_shapes=[pltpu.VMEM((B, L + SUBLANES, tn), F32),
                            pltpu.VMEM((B, L + SUBLANES, tn), F32)],
            compiler_params=_params("arbitrary"),
            name="conv_ffn_up_short",
        )(xn, w_a, w_v, cwa, cwv, cba, cbv, hist2)
    s_conv = jnp.concatenate([tails[0], tails[1]], axis=-1)
    return g, s_conv


def _ple_kernel(x_ref, wg_ref, p_ref, wp_ref, h_ref, o_ref):
    gate = jax.nn.sigmoid(_dot(x_ref[...], wg_ref[...]))
    emb = _dot(p_ref[...].astype(BF16), wp_ref[...])
    o_ref[...] = h_ref[...] + gate * emb


def ple_update(xn, w_pg, p, w_ple, h, *, tm, tn):
    M, D = xn.shape
    return pl.pallas_call(
        _ple_kernel,
        grid=(M // tm, D // tn),
        in_specs=[pl.BlockSpec((tm, D), lambda i, j: (i, 0)),
                  pl.BlockSpec((D, tn), lambda i, j: (0, j)),
                  pl.BlockSpec((tm, PLE_DIM), lambda i, j: (i, 0)),
                  pl.BlockSpec((PLE_DIM, tn), lambda i, j: (0, j)),
                  pl.BlockSpec((tm, tn), lambda i, j: (i, j))],
        out_specs=pl.BlockSpec((tm, tn), lambda i, j: (i, j)),
        out_shape=jax.ShapeDtypeStruct((M, D), F32),
        input_output_aliases={4: 0},
        compiler_params=_params("parallel", "arbitrary"),
        name="ple_update",
    )(xn, w_pg, p, w_ple, h)


def _layer_stack(x, p, st_pool, st_hg, st_s5r, st_s5i, st_conv, pos0, lb, wts):
    B, L, D = x.shape
    T = B * L
    W = W_BRANCH
    tm = min(T, 1024)
    h = x.reshape(T, D)
    n_pool, n_hg, n_s5r, n_s5i, n_conv = [], [], [], [], []
    for l in range(DEPTH):
        w = wts[l]
        xn = rms_norm(h, w['norm_mix'], BF16)
        proj = matmul(xn, w['w_in'], F32, tm=tm, tn=512, col_block0=0, n_cols=N_MIX, name="in_proj_mix")
        gates = matmul(xn, w['w_in'], BF16, tm=tm, tn=512, col_block0=N_MIX // 512,
                       n_cols=N_BRANCH * D, name="in_proj_gates")
        o_a, s_pool = pool_mixer(proj, st_pool[l], w['pool_w'], w['pool_scale'], B=B, L=L, pos0=pos0)
        o_b, s_hg = hgrn2_mixer(proj, lb[l], w['hg_norm'], st_hg[l], B=B, L=L)
        y_c, s_r, s_i = s5_mixer(proj, st_s5r[l], st_s5i[l], w['s5_disc'], w['s5_d'], B=B, L=L)
        o_c = s5_glu(y_c, w['s5_w_glu'], w['s5_b_glu'], tm=tm, tn=512)
        merged = branch_merge(o_a, o_b, o_c, w['w_br'], gates, tm=min(T, 512), tn=512)
        h = matmul_residual(merged, w['w_out'], h, tm=tm, tn=512, name="out_proj")
        xn = rms_norm(h, w['norm_ffn'], BF16)
        g, s_conv = conv_ffn_up(xn, w['w_up_a'], w['w_up_v'], w['conv_w'], w['conv_b'], st_conv[l], B=B, L=L)
        h = matmul_residual(g, w['w_down'], h, tm=min(T, 512), tn=256, name="ffn_down")
        xn = rms_norm(h, w['norm_ple'], BF16)
        h = ple_update(xn, w['w_ple_gate'], p[l].reshape(T, PLE_DIM), w['w_ple'], h, tm=tm, tn=512)
        n_pool.append(s_pool)
        n_hg.append(s_hg)
        n_s5r.append(s_r)
        n_s5i.append(s_i)
        n_conv.append(s_conv)
    y = rms_norm(h, wts[0]['norm_final'], F32).reshape(B, L, D)
    return (y, jnp.stack(n_pool), jnp.stack(n_hg), jnp.stack(n_s5r), jnp.stack(n_s5i), jnp.stack(n_conv))


def kernel(x_prompt, x_sample, state_pool, state_hgrn, state_s5_re, state_s5_im, state_ffn_conv,
           p_prompt, p_sample, norm_mix, w_in, pool_w, pool_scale, hg_lb_logits, hg_norm,
           s5_a_re, s5_a_im, s5_log_step, s5_b_re, s5_b_im, s5_c_re, s5_c_im, s5_d, s5_w_glu, s5_b_glu,
           w_br, w_out, norm_ffn, w_up, conv_w, conv_b, w_down, norm_ple, w_ple_gate, w_ple, norm_final):
    lb = forget_lower_bounds(hg_lb_logits)
    wts = []
    for l in range(DEPTH):
        wts.append({
            'norm_mix': norm_mix[l], 'w_in': w_in[l].astype(BF16), 'pool_w': pool_w[l].astype(BF16),
            'pool_scale': pool_scale[l], 'hg_norm': hg_norm[l],
            's5_disc': _s5_discretize(s5_a_re[l], s5_a_im[l], s5_log_step[l], s5_b_re[l], s5_b_im[l],
                                      s5_c_re[l], s5_c_im[l]),
            's5_d': s5_d[l], 's5_w_glu': s5_w_glu[l].astype(BF16), 's5_b_glu': s5_b_glu[l],
            'w_br': w_br[l].astype(BF16), 'w_out': w_out[l].astype(BF16), 'norm_ffn': norm_ffn[l],
            'w_up_a': w_up[l, :, :D_FF].astype(BF16), 'w_up_v': w_up[l, :, D_FF:].astype(BF16),
            'conv_w': conv_w[l], 'conv_b': conv_b[l], 'w_down': w_down[l].astype(BF16),
            'norm_ple': norm_ple[l], 'w_ple_gate': w_ple_gate[l].astype(BF16),
            'w_ple': w_ple[l].astype(BF16), 'norm_final': norm_final,
        })
    bp = x_prompt.shape[0]
    z_pool = jnp.zeros((DEPTH, bp) + state_pool.shape[2:], F32)
    z_hg = jnp.zeros((DEPTH, bp) + state_hgrn.shape[2:], F32)
    z_s5 = jnp.zeros((DEPTH, bp) + state_s5_re.shape[2:], F32)
    z_conv = jnp.zeros((DEPTH, bp) + state_ffn_conv.shape[2:], F32)
    y_p, pool_p, hg_p, s5r_p, s5i_p, conv_p = _layer_stack(
        x_prompt, p_prompt, z_pool, z_hg, z_s5, z_s5, z_conv, 0, lb, wts)
    y_s, pool_s, hg_s, s5r_s, s5i_s, conv_s = _layer_stack(
        x_sample, p_sample, state_pool, state_hgrn, state_s5_re, state_s5_im, state_ffn_conv,
        PAST_LEN, lb, wts)
    return (y_p, y_s, pool_p, hg_p, s5r_p, s5i_p, conv_p, pool_s, hg_s, s5r_s, s5i_s, conv_s)
```

```python
import functools

import jax
import jax.numpy as jnp
from jax import lax
from jax.experimental import pallas as pl
from jax.experimental.pallas import tpu as pltpu

D_MODEL = 4096
DEPTH = 4
PAST_LEN = 1024
W_BRANCH = D_MODEL // 2
N_BRANCH = 3
POOL_WINDOWS = (2, 4, 8, 16)
POOL_GROUP = W_BRANCH // len(POOL_WINDOWS)
POOL_HIST = max(POOL_WINDOWS) - 1
HG_DK = 128
HG_HEADS = W_BRANCH // HG_DK
HG_DV = W_BRANCH // HG_HEADS
HG_BLOCK = 16
S5_GROUP = 16
S5_GROUPS = W_BRANCH // S5_GROUP
S5_STATE = 64
D_FF = 11008
CONV_W = 3
PLE_DIM = 256
N_MIX = 6 * W_BRANCH
N_IN = N_MIX + N_BRANCH * D_MODEL
EPS = 1e-6

F32 = jnp.float32
BF16 = jnp.bfloat16

VMEM_LIMIT_BYTES = 56 * 1024 * 1024
SUBLANES = 8
BF16_ROWS = 16

S5_CHUNK_GROUPS = 16
S5_CHUNKS = S5_GROUPS // S5_CHUNK_GROUPS
S5_CH = S5_CHUNK_GROUPS * S5_GROUP
S5_ST = S5_CHUNK_GROUPS * S5_STATE


def _params(*sem, flags=None):
    return pltpu.CompilerParams(dimension_semantics=sem, vmem_limit_bytes=VMEM_LIMIT_BYTES, flags=flags)


def _dot(a, b):
    return jnp.dot(a, b, preferred_element_type=F32)


def _norm_kernel(x_ref, g_ref, o_ref):
    x = x_ref[...]
    ms = jnp.mean(x * x, axis=-1, keepdims=True)
    o_ref[...] = ((x * lax.rsqrt(ms + EPS)) * g_ref[...]).astype(o_ref.dtype)


def rms_norm(x, g, out_dtype):
    T, D = x.shape
    tr = min(T, 512)
    return pl.pallas_call(
        _norm_kernel,
        grid=(T // tr,),
        in_specs=[pl.BlockSpec((tr, D), lambda i: (i, 0)),
                  pl.BlockSpec((1, D), lambda i: (0, 0))],
        out_specs=pl.BlockSpec((tr, D), lambda i: (i, 0)),
        out_shape=jax.ShapeDtypeStruct((T, D), out_dtype),
        compiler_params=_params("parallel"),
        name="rms_norm",
    )(x, g.reshape(1, D))


def _mm_kernel(a_ref, b_ref, o_ref):
    o_ref[...] = _dot(a_ref[...], b_ref[...]).astype(o_ref.dtype)


def matmul(a, b, out_dtype, *, tm, tn, col_block0, n_cols, name):
    M, K = a.shape
    return pl.pallas_call(
        _mm_kernel,
        grid=(M // tm, n_cols // tn),
        in_specs=[pl.BlockSpec((tm, K), lambda i, j: (i, 0)),
                  pl.BlockSpec((K, tn), lambda i, j: (0, j + col_block0))],
        out_specs=pl.BlockSpec((tm, tn), lambda i, j: (i, j)),
        out_shape=jax.ShapeDtypeStruct((M, n_cols), out_dtype),
        compiler_params=_params("parallel", "arbitrary"),
        name=name,
    )(a, b)


def _mm_res_kernel(a_ref, b_ref, h_ref, o_ref):
    o_ref[...] = h_ref[...] + _dot(a_ref[...], b_ref[...])


def matmul_residual(a, b, h, *, tm, tn, name):
    M, K = a.shape
    N = b.shape[1]
    return pl.pallas_call(
        _mm_res_kernel,
        grid=(M // tm, N // tn),
        in_specs=[pl.BlockSpec((tm, K), lambda i, j: (i, 0)),
                  pl.BlockSpec((K, tn), lambda i, j: (0, j)),
                  pl.BlockSpec((tm, tn), lambda i, j: (i, j))],
        out_specs=pl.BlockSpec((tm, tn), lambda i, j: (i, j)),
        out_shape=jax.ShapeDtypeStruct((M, N), F32),
        input_output_aliases={2: 0},
        compiler_params=_params("parallel", "arbitrary"),
        name=name,
    )(a, b, h)


def _lb_kernel(x_ref, o_ref):
    x = x_ref[...]
    m = jnp.max(x, axis=0, keepdims=True)
    e = jnp.exp(x - m)
    p = e / jnp.sum(e, axis=0, keepdims=True)
    rows = [p[0:1]]
    for l in range(1, DEPTH):
        rows.append(rows[-1] + p[l:l + 1])
    o_ref[...] = jnp.concatenate([r - rows[0] for r in rows], axis=0)


def forget_lower_bounds(logits):
    return pl.pallas_call(
        _lb_kernel,
        out_shape=jax.ShapeDtypeStruct(logits.shape, F32),
        name="hgrn_lower_bounds",
    )(logits)


def _pool_kernel(u_ref, prev_ref, hist_ref, w_ref, scale_ref, o_ref, st_ref, ext_ref, *, tt, pos0):
    ti = pl.program_id(1)
    nt = pl.num_programs(1)
    hist_rows = POOL_HIST + 1

    @pl.when(ti == 0)
    def _():
        ext_ref[0:hist_rows, :] = hist_ref[0]

    @pl.when(ti > 0)
    def _():
        ext_ref[0:hist_rows, :] = prev_ref[...]

    cur = u_ref[...]
    ext_ref[hist_rows:, :] = cur
    pos = pos0 + ti * tt + lax.broadcasted_iota(jnp.int32, (tt, 1), 0)
    for gi, w in enumerate(POOL_WINDOWS):
        cols = slice(gi * POOL_GROUP, (gi + 1) * POOL_GROUP)
        win = ext_ref[pl.ds(hist_rows, tt), cols]
        for k in range(1, w):
            win = win + ext_ref[pl.ds(hist_rows - k, tt), cols]
        cnt = jnp.minimum(pos + 1, w).astype(F32)
        pooled = win / cnt - cur[:, cols]
        y = _dot(pooled.astype(BF16), w_ref[gi]) * scale_ref[:, cols]
        o_ref[:, cols] = y.astype(o_ref.dtype)

    @pl.when(ti == nt - 1)
    def _():
        st_ref[0] = ext_ref[pl.ds(tt + 1, POOL_HIST), :]


def pool_mixer(proj, hist, w_bf16, scale, *, B, L, pos0):
    W = W_BRANCH
    tt = min(L, 256)
    nt = L // tt
    hist16 = jnp.concatenate([jnp.zeros((B, 1, W), F32), hist], axis=1)
    blk16 = tt // (POOL_HIST + 1)
    kern = functools.partial(_pool_kernel, tt=tt, pos0=pos0)
    return pl.pallas_call(
        kern,
        grid=(B, nt),
        in_specs=[pl.BlockSpec((tt, W), lambda b, t: (b * nt + t, 0)),
                  pl.BlockSpec((POOL_HIST + 1, W),
                               lambda b, t: (jnp.maximum((b * nt + t) * blk16 - 1, 0), 0)),
                  pl.BlockSpec((1, POOL_HIST + 1, W), lambda b, t: (b, 0, 0)),
                  pl.BlockSpec((len(POOL_WINDOWS), POOL_GROUP, POOL_GROUP), lambda b, t: (0, 0, 0)),
                  pl.BlockSpec((1, W), lambda b, t: (0, 0))],
        out_specs=[pl.BlockSpec((tt, W), lambda b, t: (b * nt + t, 0)),
                   pl.BlockSpec((1, POOL_HIST, W), lambda b, t: (b, 0, 0))],
        out_shape=[jax.ShapeDtypeStruct((B * L, W), BF16),
                   jax.ShapeDtypeStruct((B, POOL_HIST, W), F32)],
        scratch_shapes=[pltpu.VMEM((tt + POOL_HIST + 1, W), F32)],
        compiler_params=_params("parallel", "arbitrary"),
        name="pool_mixer",
    )(proj, proj, hist16, w_bf16, scale.reshape(1, W))


HG_HEADS_PER_STEP = 16


def _hgrn_kernel(q_ref, f_ref, v_ref, g_ref, lb_ref, ng_ref, s0_ref, o_ref, s_ref, st_ref, oacc_ref, *, C):
    ci = pl.program_id(2)
    nc = pl.num_programs(2)
    hb = HG_HEADS_PER_STEP
    nsub = C // HG_BLOCK
    ref_row = HG_BLOCK // 2 - 1

    @pl.when(ci == 0)
    def _():
        for h in range(hb):
            st_ref[h] = s0_ref[0, h].T

    lb = lb_ref[...]
    f = lb + (1.0 - lb) * jax.nn.sigmoid(f_ref[...])
    logf = jnp.log(f)
    kk = 1.0 - f
    tri = (lax.broadcasted_iota(jnp.int32, (C, C), 1)
           <= lax.broadcasted_iota(jnp.int32, (C, C), 0)).astype(F32)
    bcum = jnp.dot(tri, logf, precision=lax.Precision.HIGHEST, preferred_element_type=F32)
    blast = bcum[C - 1:C, :]
    q = q_ref[...]
    v = v_ref[...]
    qd = (q * jnp.exp(bcum)).astype(BF16)
    kdec = (kk * jnp.exp(blast - bcum)).astype(BF16)
    eb = jnp.exp(blast)
    vb = v.astype(BF16)
    gate = jax.nn.silu(g_ref[...])
    ng = ng_ref[...]
    for h in range(hb):
        sl = slice(h * HG_DK, (h + 1) * HG_DK)
        st = st_ref[h]
        oacc_ref[...] = lax.dot_general(qd[:, sl], st.astype(BF16), (((1,), (1,)), ((), ())),
                                        preferred_element_type=F32)
        for j in range(nsub):
            r0 = j * HG_BLOCK
            bj = bcum[r0:, sl]
            bref = bcum[r0 + ref_row:r0 + ref_row + 1, sl]
            qj = (q[r0:, sl] * jnp.exp(bj - bref)).astype(BF16)
            kj = (kk[r0:r0 + HG_BLOCK, sl] * jnp.exp(bref - bj[:HG_BLOCK])).astype(BF16)
            att = lax.dot_general(qj, kj, (((1,), (1,)), ((), ())), preferred_element_type=F32)
            rr = lax.broadcasted_iota(jnp.int32, att.shape, 0)
            cc = lax.broadcasted_iota(jnp.int32, att.shape, 1)
            att = jnp.where(cc <= rr, att, 0.0)
            oacc_ref[r0:, :] += _dot(att.astype(BF16), vb[r0:r0 + HG_BLOCK, sl])
        o = oacc_ref[...]
        o = o * lax.rsqrt(jnp.mean(o * o, axis=-1, keepdims=True) + EPS)
        o = o * ng[:, sl]
        o_ref[:, sl] = (o * gate[:, sl]).astype(o_ref.dtype)
        upd = lax.dot_general(vb[:, sl], kdec[:, sl], (((0,), (0,)), ((), ())),
                              preferred_element_type=F32)
        st_ref[h] = eb[:, sl] * st + upd

    @pl.when(ci == nc - 1)
    def _():
        for h in range(hb):
            s_ref[0, h] = st_ref[h].T


def hgrn2_mixer(proj, lb, norm_g, s0, *, B, L):
    W = W_BRANCH
    C = min(L, 128)
    nc = L // C
    hb = HG_HEADS_PER_STEP
    nh = HG_HEADS // hb
    wb = hb * HG_DK
    per = W // wb

    def col(k):
        return pl.BlockSpec((C, wb), lambda b, h, c: (b * nc + c, k * per + h))

    vec = pl.BlockSpec((1, wb), lambda b, h, c: (0, h))
    st = pl.BlockSpec((1, hb, HG_DK, HG_DV), lambda b, h, c: (b, h, 0, 0))
    kern = functools.partial(_hgrn_kernel, C=C)
    return pl.pallas_call(
        kern,
        grid=(B, nh, nc),
        in_specs=[col(1), col(2), col(3), col(4), vec, vec, st],
        out_specs=[pl.BlockSpec((C, wb), lambda b, h, c: (b * nc + c, h)), st],
        out_shape=[jax.ShapeDtypeStruct((B * L, W), BF16),
                   jax.ShapeDtypeStruct((B, HG_HEADS, HG_DK, HG_DV), F32)],
        scratch_shapes=[pltpu.VMEM((hb, HG_DV, HG_DK), F32),
                        pltpu.VMEM((C, HG_DV), F32)],
        compiler_params=_params("parallel", "parallel", "arbitrary"),
        name="hgrn2_mixer",
    )(proj, proj, proj, proj, lb.reshape(1, W), norm_g.reshape(1, W), s0)


S5_CHUNKS_PER_STEP = 4


def _s5_scan_tile(xr, xi, cr, cim, pw_ref, cc):
    for idx, k in enumerate((1, 2, 4)):
        sr = pltpu.roll(xr, k, 0)
        si = pltpu.roll(xi, k, 0)
        mr = pw_ref[cc, 2 * idx]
        mi = pw_ref[cc, 2 * idx + 1]
        xr, xi = xr + (mr * sr - mi * si), xi + (mr * si + mi * sr)
    pr = pw_ref[cc, 6]
    pim = pw_ref[cc, 7]
    return xr + (pr * cr - pim * cim), xi + (pr * cim + pim * cr)


def _s5_kernel(u_ref, bw_ref, cw_ref, pw_ref, x0_ref, d_ref, y_ref, xo_ref,
               xr0_s, xi0_s, xr1_s, xi1_s, car_s, *, nseq, seg, nt):
    ti = pl.program_id(2)
    bufs = ((xr0_s, xi0_s), (xr1_s, xi1_s))

    def bcast(row):
        return jnp.broadcast_to(row, (SUBLANES, S5_ST))

    if nt > 1:
        @pl.when(ti == 0)
        def _():
            for cc in range(S5_CHUNKS_PER_STEP):
                car_s[cc, 0] = bcast(x0_ref[0, cc, 0:1, :])
                car_s[cc, 1] = bcast(x0_ref[0, cc, 1:2, :])

    for cc in range(S5_CHUNKS_PER_STEP):
        xr_s, xi_s = bufs[cc % 2]
        lanes = slice(cc * S5_CH, (cc + 1) * S5_CH)
        u = u_ref[:, lanes]
        bu = _dot(u.astype(BF16), bw_ref[cc])
        xr_s[...] = bu[:, :S5_ST]
        xi_s[...] = bu[:, S5_ST:]
        for q in range(nseq):
            if nt > 1:
                cr, cim = car_s[cc, 0], car_s[cc, 1]
            else:
                cr, cim = bcast(x0_ref[q, cc, 0:1, :]), bcast(x0_ref[q, cc, 1:2, :])
            for r in range(seg // SUBLANES):
                rows = slice(q * seg + r * SUBLANES, q * seg + (r + 1) * SUBLANES)
                xr, xi = _s5_scan_tile(xr_s[rows, :], xi_s[rows, :], cr, cim, pw_ref, cc)
                xr_s[rows, :] = xr
                xi_s[rows, :] = xi
                cr, cim = bcast(xr[SUBLANES - 1:SUBLANES]), bcast(xi[SUBLANES - 1:SUBLANES])
            if nt > 1:
                car_s[cc, 0] = cr
                car_s[cc, 1] = cim
            else:
                xo_ref[q, cc] = jnp.concatenate([cr[0:1], cim[0:1]], axis=0)
        y = _dot(xr_s[...].astype(BF16), cw_ref[cc, 0]) + _dot(xi_s[...].astype(BF16), cw_ref[cc, 1])
        y = y + d_ref[:, lanes] * u
        y_ref[:, lanes] = jax.nn.gelu(y).astype(y_ref.dtype)

    if nt > 1:
        @pl.when(ti == nt - 1)
        def _():
            for cc in range(S5_CHUNKS_PER_STEP):
                xo_ref[0, cc] = jnp.concatenate([car_s[cc, 0, 0:1, :], car_s[cc, 1, 0:1, :]], axis=0)


def _s5_discretize(a_re, a_im, log_step, b_re, b_im, c_re, c_im):
    dt = jnp.exp(log_step)[:, None]
    mag = jnp.exp(dt * a_re)
    ab_re = mag * jnp.cos(dt * a_im)
    ab_im = mag * jnp.sin(dt * a_im)
    den = a_re * a_re + a_im * a_im
    coef_re = ((ab_re - 1.0) * a_re + ab_im * a_im) / den
    coef_im = (ab_im * a_re - (ab_re - 1.0) * a_im) / den
    bb_re = coef_re[..., None] * b_re - coef_im[..., None] * b_im
    bb_im = coef_re[..., None] * b_im + coef_im[..., None] * b_re
    eye = jnp.eye(S5_CHUNK_GROUPS, dtype=F32)

    def blockdiag_b(bb):
        t = bb.reshape(S5_CHUNKS, S5_CHUNK_GROUPS, S5_STATE, S5_GROUP).transpose(0, 1, 3, 2)
        return jnp.einsum('cgxp,gh->cgxhp', t, eye).reshape(S5_CHUNKS, S5_CH, S5_ST)

    def blockdiag_c(cc):
        t = cc.reshape(S5_CHUNKS, S5_CHUNK_GROUPS, S5_GROUP, S5_STATE)
        return jnp.einsum('cgxp,gh->cgphx', t, eye).reshape(S5_CHUNKS, S5_ST, S5_CH)

    bw = jnp.concatenate([blockdiag_b(bb_re), blockdiag_b(bb_im)], axis=2).astype(BF16)
    cw = jnp.stack([blockdiag_c(c_re), -blockdiag_c(c_im)], axis=1).astype(BF16)

    pows = [(ab_re, ab_im)]
    for _ in range(SUBLANES - 1):
        pr, pi = pows[-1]
        pows.append((pr * ab_re - pi * ab_im, pr * ab_im + pi * ab_re))

    def plane(vals):
        return jnp.stack([v.reshape(S5_CHUNKS, S5_ST) for v in vals], axis=1)

    zero = jnp.zeros_like(ab_re)
    planes = []
    for k in (1, 2, 4):
        planes.append(plane([pows[k - 1][0] if t >= k else zero for t in range(SUBLANES)]))
        planes.append(plane([pows[k - 1][1] if t >= k else zero for t in range(SUBLANES)]))
    planes.append(plane([pows[t][0] for t in range(SUBLANES)]))
    planes.append(plane([pows[t][1] for t in range(SUBLANES)]))
    pw = jnp.stack(planes, axis=1)
    return bw, cw, pw


def s5_mixer(proj, x0_re, x0_im, disc, d_skip, *, B, L):
    W = W_BRANCH
    bw, cw, pw = disc
    cps = S5_CHUNKS_PER_STEP
    if L >= 512:
        nseq, seg, nt, bsteps = 1, 512, L // 512, B
    else:
        nseq, seg, nt, bsteps = B, L, 1, 1
    rows = nseq * seg
    col0 = 5 * W // (cps * S5_CH)
    x0 = jnp.stack([x0_re.reshape(B, S5_CHUNKS, S5_ST), x0_im.reshape(B, S5_CHUNKS, S5_ST)], axis=2)
    kern = functools.partial(_s5_kernel, nseq=nseq, seg=seg, nt=nt)
    state = pl.BlockSpec((nseq, cps, 2, S5_ST), lambda b, c, t: (b, c, 0, 0))
    y, xo = pl.pallas_call(
        kern,
        grid=(bsteps, S5_CHUNKS // cps, nt),
        in_specs=[pl.BlockSpec((rows, cps * S5_CH), lambda b, c, t: (b * nt + t, col0 + c)),
                  pl.BlockSpec((cps, S5_CH, 2 * S5_ST), lambda b, c, t: (c, 0, 0)),
                  pl.BlockSpec((cps, 2, S5_ST, S5_CH), lambda b, c, t: (c, 0, 0, 0)),
                  pl.BlockSpec((cps, 8, SUBLANES, S5_ST), lambda b, c, t: (c, 0, 0, 0)),
                  state,
                  pl.BlockSpec((1, cps * S5_CH), lambda b, c, t: (0, c))],
        out_specs=[pl.BlockSpec((rows, cps * S5_CH), lambda b, c, t: (b * nt + t, c)), state],
        out_shape=[jax.ShapeDtypeStruct((B * L, W), BF16),
                   jax.ShapeDtypeStruct((B, S5_CHUNKS, 2, S5_ST), F32)],
        scratch_shapes=[pltpu.VMEM((rows, S5_ST), F32), pltpu.VMEM((rows, S5_ST), F32),
                        pltpu.VMEM((rows, S5_ST), F32), pltpu.VMEM((rows, S5_ST), F32),
                        pltpu.VMEM((cps, 2, SUBLANES, S5_ST), F32)],
        compiler_params=_params("parallel", "parallel", "arbitrary"),
        name="s5_scan",
    )(proj, bw, cw, pw, x0, d_skip.reshape(1, W))
    s_re = xo[:, :, 0, :].reshape(B, S5_GROUPS, S5_STATE)
    s_im = xo[:, :, 1, :].reshape(B, S5_GROUPS, S5_STATE)
    return y, s_re, s_im


def _glu_kernel(a_ref, b_ref, y_ref, bias_ref, o_ref):
    z = _dot(a_ref[...], b_ref[...]) + bias_ref[...]
    o_ref[...] = (y_ref[...].astype(F32) * jax.nn.sigmoid(z)).astype(o_ref.dtype)


def s5_glu(y, w_bf16, bias, *, tm, tn):
    M, K = y.shape
    N = w_bf16.shape[1]
    return pl.pallas_call(
        _glu_kernel,
        grid=(M // tm, N // tn),
        in_specs=[pl.BlockSpec((tm, K), lambda i, j: (i, 0)),
                  pl.BlockSpec((K, tn), lambda i, j: (0, j)),
                  pl.BlockSpec((tm, tn), lambda i, j: (i, j)),
                  pl.BlockSpec((1, tn), lambda i, j: (0, j))],
        out_specs=pl.BlockSpec((tm, tn), lambda i, j: (i, j)),
        out_shape=jax.ShapeDtypeStruct((M, N), BF16),
        compiler_params=_params("parallel", "arbitrary"),
        name="s5_glu",
    )(y, w_bf16, y, bias.reshape(1, N))


def _merge_kernel(oa_ref, ob_ref, oc_ref, w_ref, ga_ref, gb_ref, gc_ref, o_ref):
    acc = jax.nn.sigmoid(ga_ref[...].astype(F32)) * _dot(oa_ref[...], w_ref[0])
    acc = acc + jax.nn.sigmoid(gb_ref[...].astype(F32)) * _dot(ob_ref[...], w_ref[1])
    acc = acc + jax.nn.sigmoid(gc_ref[...].astype(F32)) * _dot(oc_ref[...], w_ref[2])
    o_ref[...] = acc.astype(o_ref.dtype)


def branch_merge(o_a, o_b, o_c, w_br_bf16, gates, *, tm, tn):
    M, W = o_a.shape
    D = D_MODEL
    nj = D // tn
    act = pl.BlockSpec((tm, W), lambda i, j: (i, 0))

    def gate(n):
        return pl.BlockSpec((tm, tn), lambda i, j: (i, n * nj + j))

    return pl.pallas_call(
        _merge_kernel,
        grid=(M // tm, nj),
        in_specs=[act, act, act,
                  pl.BlockSpec((N_BRANCH, W, tn), lambda i, j: (0, 0, j)),
                  gate(0), gate(1), gate(2)],
        out_specs=pl.BlockSpec((tm, tn), lambda i, j: (i, j)),
        out_shape=jax.ShapeDtypeStruct((M, D), BF16),
        compiler_params=_params("parallel", "arbitrary"),
        name="branch_merge",
    )(o_a, o_b, o_c, w_br_bf16, gates, gates, gates)


def _conv_gate(ua0, ua1, ua2, uv0, uv1, uv2, cwa_ref, cwv_ref, cba_ref, cbv_ref):
    shape = ua0.shape
    tn = shape[-1]

    def tiles(x):
        return x.reshape(-1, SUBLANES, tn)

    def tap(ref, k):
        return ref[k * SUBLANES:(k + 1) * SUBLANES, :][None]

    a = cba_ref[...][None] + (tap(cwa_ref, 0) * tiles(ua0) + tap(cwa_ref, 1) * tiles(ua1) + tap(cwa_ref, 2) * tiles(ua2))
    v = cbv_ref[...][None] + (tap(cwv_ref, 0) * tiles(uv0) + tap(cwv_ref, 1) * tiles(uv1) + tap(cwv_ref, 2) * tiles(uv2))
    return (jax.nn.gelu(a) * v).reshape(shape)


CONV_ROWS_PER_ITER = 64


def _up_conv_long_kernel(x_ref, xp_ref, wal_ref, wah_ref, wvl_ref, wvh_ref, cwa_ref, cwv_ref, cba_ref, cbv_ref,
                         hist_ref, g_ref, tail_ref, xe_s, ua_s, uv_s, *, tm, tiles_per_seq):
    i = pl.program_id(0)
    j = pl.program_id(1)
    pad = BF16_ROWS
    half = wal_ref.shape[1]

    @pl.when(j == 0)
    def _():
        xe_s[0:pad, :] = xp_ref[...]
        xe_s[pad:, :] = x_ref[...]

    xe = xe_s[...]
    ua_s[:, :half] = _dot(xe, wal_ref[...])
    ua_s[:, half:] = _dot(xe, wah_ref[...])
    uv_s[:, :half] = _dot(xe, wvl_ref[...])
    uv_s[:, half:] = _dot(xe, wvh_ref[...])

    @pl.when(i % tiles_per_seq == 0)
    def _():
        ua_s[pad - 2:pad, :] = hist_ref[0, 0]
        uv_s[pad - 2:pad, :] = hist_ref[1, 0]

    rows = CONV_ROWS_PER_ITER

    def shifted(u_s, r0):
        x = u_s[pl.ds(r0 + pad - SUBLANES, rows + SUBLANES), :]
        return (pltpu.roll(x, 2, 0)[SUBLANES:], pltpu.roll(x, 1, 0)[SUBLANES:], x[SUBLANES:])

    def body(it, carry):
        r0 = pl.multiple_of(it * rows, rows)
        g = _conv_gate(*shifted(ua_s, r0), *shifted(uv_s, r0), cwa_ref, cwv_ref, cba_ref, cbv_ref)
        g_ref[pl.ds(r0, rows), :] = g.astype(g_ref.dtype)
        return carry

    lax.fori_loop(0, tm // rows, body, 0)
    tail_ref[0, 0] = ua_s[pl.ds(pad + tm - 2, 2), :]
    tail_ref[1, 0] = uv_s[pl.ds(pad + tm - 2, 2), :]


def _up_conv_short_kernel(x_ref, wal_ref, wah_ref, wvl_ref, wvh_ref, cwa_ref, cwv_ref, cba_ref, cbv_ref, hist_ref,
                          g_ref, tail_ref, ua_s, uv_s, *, nseq, L):
    x = x_ref[...]
    half = wal_ref.shape[1]
    tn = 2 * half
    pad = SUBLANES
    ua_s[:, pad:, :half] = _dot(x, wal_ref[...]).reshape(nseq, L, half)
    ua_s[:, pad:, half:] = _dot(x, wah_ref[...]).reshape(nseq, L, half)
    uv_s[:, pad:, :half] = _dot(x, wvl_ref[...]).reshape(nseq, L, half)
    uv_s[:, pad:, half:] = _dot(x, wvh_ref[...]).reshape(nseq, L, half)
    ua_s[:, pad - 2:pad, :] = hist_ref[0]
    uv_s[:, pad - 2:pad, :] = hist_ref[1]
    g = _conv_gate(ua_s[:, pad - 2:pad - 2 + L, :], ua_s[:, pad - 1:pad - 1 + L, :], ua_s[:, pad:pad + L, :],
                   uv_s[:, pad - 2:pad - 2 + L, :], uv_s[:, pad - 1:pad - 1 + L, :], uv_s[:, pad:pad + L, :],
                   cwa_ref, cwv_ref, cba_ref, cbv_ref)
    g_ref[...] = g.reshape(nseq * L, tn).astype(g_ref.dtype)
    tail_ref[0] = ua_s[:, pad + L - 2:pad + L, :]
    tail_ref[1] = uv_s[:, pad + L - 2:pad + L, :]


def conv_ffn_up(xn, w_up, conv_w, conv_b, hist, *, B, L):
    T, D = xn.shape
    tn = 512
    half = tn // 2
    nj = pl.cdiv(D_FF, tn)
    voff = D_FF // half
    last = 2 * D_FF // half - 1
    cw8 = jnp.repeat(conv_w, SUBLANES, axis=0)
    cb8 = jnp.broadcast_to(conv_b[None, :], (SUBLANES, 2 * D_FF))
    cwa, cwv = cw8[:, :D_FF], cw8[:, D_FF:]
    cba, cbv = cb8[:, :D_FF], cb8[:, D_FF:]
    hist2 = jnp.stack([hist[:, :, :D_FF], hist[:, :, D_FF:]], axis=0)
    if L >= 1024:
        tm = 1024
        tps = L // tm
        nti = T // tm

        def wspec(k):
            return pl.BlockSpec((D, half), lambda i, j: (0, jnp.minimum(2 * j + k, last)))

        cwspec = pl.BlockSpec((CONV_W * SUBLANES, tn), lambda i, j: (0, j))
        cbspec = pl.BlockSpec((SUBLANES, tn), lambda i, j: (0, j))
        kern = functools.partial(_up_conv_long_kernel, tm=tm, tiles_per_seq=tps)
        g, tails = pl.pallas_call(
            kern,
            grid=(nti, nj),
            in_specs=[pl.BlockSpec((tm, D), lambda i, j: (i, 0)),
                      pl.BlockSpec((BF16_ROWS, D),
                                   lambda i, j: (jnp.maximum(i * (tm // BF16_ROWS) - 1, 0), 0)),
                      wspec(0), wspec(1), wspec(voff), wspec(voff + 1),
                      cwspec, cwspec, cbspec, cbspec,
                      pl.BlockSpec((2, 1, 2, tn), lambda i, j: (0, i // tps, 0, j))],
            out_specs=[pl.BlockSpec((tm, tn), lambda i, j: (i, j)),
                       pl.BlockSpec((2, 1, 2, tn), lambda i, j: (0, i, 0, j))],
            out_shape=[jax.ShapeDtypeStruct((T, D_FF), BF16),
                       jax.ShapeDtypeStruct((2, nti, 2, D_FF), F32)],
            scratch_shapes=[pltpu.VMEM((tm + BF16_ROWS, D), BF16),
                            pltpu.VMEM((tm + BF16_ROWS, tn), F32),
                            pltpu.VMEM((tm + BF16_ROWS, tn), F32)],
            compiler_params=_params("parallel", "arbitrary"),
            name="conv_ffn_up_long",
        )(xn, xn, w_up, w_up, w_up, w_up, cwa, cwv, cba, cbv, hist2)
        tails = tails[:, tps - 1::tps]
    else:
        def wspec(k):
            return pl.BlockSpec((D, half), lambda j: (0, jnp.minimum(2 * j + k, last)))

        kern = functools.partial(_up_conv_short_kernel, nseq=B, L=L)
        g, tails = pl.pallas_call(
            kern,
            grid=(nj,),
            in_specs=[pl.BlockSpec((T, D), lambda j: (0, 0)),
                      wspec(0), wspec(1), wspec(voff), wspec(voff + 1),
                      pl.BlockSpec((CONV_W * SUBLANES, tn), lambda j: (0, j)),
                      pl.BlockSpec((CONV_W * SUBLANES, tn), lambda j: (0, j)),
                      pl.BlockSpec((SUBLANES, tn), lambda j: (0, j)),
                      pl.BlockSpec((SUBLANES, tn), lambda j: (0, j)),
                      pl.BlockSpec((2, B, 2, tn), lambda j: (0, 0, 0, j))],
            out_specs=[pl.BlockSpec((T, tn), lambda j: (0, j)),
                       pl.BlockSpec((2, B, 2, tn), lambda j: (0, 0, 0, j))],
            out_shape=[jax.ShapeDtypeStruct((T, D_FF), BF16),
                       jax.ShapeDtypeStruct((2, B, 2, D_FF), F32)],
            scratch_shapes=[pltpu.VMEM((B, L + SUBLANES, tn), F32),
                            pltpu.VMEM((B, L + SUBLANES, tn), F32)],
            compiler_params=_params("arbitrary"),
            name="conv_ffn_up_short",
        )(xn, w_up, w_up, w_up, w_up, cwa, cwv, cba, cbv, hist2)
    s_conv = jnp.concatenate([tails[0], tails[1]], axis=-1)
    return g, s_conv


def _ple_kernel(x_ref, wg_ref, p_ref, wp_ref, h_ref, o_ref):
    gate = jax.nn.sigmoid(_dot(x_ref[...], wg_ref[...]))
    emb = _dot(p_ref[...].astype(BF16), wp_ref[...])
    o_ref[...] = h_ref[...] + gate * emb


def ple_update(xn, w_pg, p, w_ple, h, *, tm, tn):
    M, D = xn.shape
    return pl.pallas_call(
        _ple_kernel,
        grid=(M // tm, D // tn),
        in_specs=[pl.BlockSpec((tm, D), lambda i, j: (i, 0)),
                  pl.BlockSpec((D, tn), lambda i, j: (0, j)),
                  pl.BlockSpec((tm, PLE_DIM), lambda i, j: (i, 0)),
                  pl.BlockSpec((PLE_DIM, tn), lambda i, j: (0, j)),
                  pl.BlockSpec((tm, tn), lambda i, j: (i, j))],
        out_specs=pl.BlockSpec((tm, tn), lambda i, j: (i, j)),
        out_shape=jax.ShapeDtypeStruct((M, D), F32),
        input_output_aliases={4: 0},
        compiler_params=_params("parallel", "arbitrary"),
        name="ple_update",
    )(xn, w_pg, p, w_ple, h)


def _layer_stack(x, p, st_pool, st_hg, st_s5r, st_s5i, st_conv, pos0, lb, wts):
    B, L, D = x.shape
    T = B * L
    W = W_BRANCH
    tm = min(T, 1024)
    h = x.reshape(T, D)
    n_pool, n_hg, n_s5r, n_s5i, n_conv = [], [], [], [], []
    for l in range(DEPTH):
        w = wts[l]
        xn = rms_norm(h, w['norm_mix'], BF16)
        proj = matmul(xn, w['w_in'], F32, tm=tm, tn=512, col_block0=0, n_cols=N_MIX, name="in_proj_mix")
        gates = matmul(xn, w['w_in'], BF16, tm=tm, tn=512, col_block0=N_MIX // 512,
                       n_cols=N_BRANCH * D, name="in_proj_gates")
        o_a, s_pool = pool_mixer(proj, st_pool[l], w['pool_w'], w['pool_scale'], B=B, L=L, pos0=pos0)
        o_b, s_hg = hgrn2_mixer(proj, lb[l], w['hg_norm'], st_hg[l], B=B, L=L)
        y_c, s_r, s_i = s5_mixer(proj, st_s5r[l], st_s5i[l], w['s5_disc'], w['s5_d'], B=B, L=L)
        o_c = s5_glu(y_c, w['s5_w_glu'], w['s5_b_glu'], tm=tm, tn=512)
        merged = branch_merge(o_a, o_b, o_c, w['w_br'], gates, tm=tm, tn=512)
        h = matmul_residual(merged, w['w_out'], h, tm=tm, tn=512, name="out_proj")
        xn = rms_norm(h, w['norm_ffn'], BF16)
        g, s_conv = conv_ffn_up(xn, w['w_up'], w['conv_w'], w['conv_b'], st_conv[l], B=B, L=L)
        h = matmul_residual(g, w['w_down'], h, tm=min(T, 512), tn=256, name="ffn_down")
        xn = rms_norm(h, w['norm_ple'], BF16)
        h = ple_update(xn, w['w_ple_gate'], p[l].reshape(T, PLE_DIM), w['w_ple'], h, tm=tm, tn=512)
        n_pool.append(s_pool)
        n_hg.append(s_hg)
        n_s5r.append(s_r)
        n_s5i.append(s_i)
        n_conv.append(s_conv)
    y = rms_norm(h, wts[0]['norm_final'], F32).reshape(B, L, D)
    return (y, jnp.stack(n_pool), jnp.stack(n_hg), jnp.stack(n_s5r), jnp.stack(n_s5i), jnp.stack(n_conv))


def kernel(x_prompt, x_sample, state_pool, state_hgrn, state_s5_re, state_s5_im, state_ffn_conv,
           p_prompt, p_sample, norm_mix, w_in, pool_w, pool_scale, hg_lb_logits, hg_norm,
           s5_a_re, s5_a_im, s5_log_step, s5_b_re, s5_b_im, s5_c_re, s5_c_im, s5_d, s5_w_glu, s5_b_glu,
           w_br, w_out, norm_ffn, w_up, conv_w, conv_b, w_down, norm_ple, w_ple_gate, w_ple, norm_final):
    lb = forget_lower_bounds(hg_lb_logits)
    wts = []
    for l in range(DEPTH):
        wts.append({
            'norm_mix': norm_mix[l], 'w_in': w_in[l].astype(BF16), 'pool_w': pool_w[l].astype(BF16),
            'pool_scale': pool_scale[l], 'hg_norm': hg_norm[l],
            's5_disc': _s5_discretize(s5_a_re[l], s5_a_im[l], s5_log_step[l], s5_b_re[l], s5_b_im[l],
                                      s5_c_re[l], s5_c_im[l]),
            's5_d': s5_d[l], 's5_w_glu': s5_w_glu[l].astype(BF16), 's5_b_glu': s5_b_glu[l],
            'w_br': w_br[l].astype(BF16), 'w_out': w_out[l].astype(BF16), 'norm_ffn': norm_ffn[l],
            'w_up': w_up[l].astype(BF16),
            'conv_w': conv_w[l], 'conv_b': conv_b[l], 'w_down': w_down[l].astype(BF16),
            'norm_ple': norm_ple[l], 'w_ple_gate': w_ple_gate[l].astype(BF16),
            'w_ple': w_ple[l].astype(BF16), 'norm_final': norm_final,
        })
    bp = x_prompt.shape[0]
    z_pool = jnp.zeros((DEPTH, bp) + state_pool.shape[2:], F32)
    z_hg = jnp.zeros((DEPTH, bp) + state_hgrn.shape[2:], F32)
    z_s5 = jnp.zeros((DEPTH, bp) + state_s5_re.shape[2:], F32)
    z_conv = jnp.zeros((DEPTH, bp) + state_ffn_conv.shape[2:], F32)
    y_p, pool_p, hg_p, s5r_p, s5i_p, conv_p = _layer_stack(
        x_prompt, p_prompt, z_pool, z_hg, z_s5, z_s5, z_conv, 0, lb, wts)
    y_s, pool_s, hg_s, s5r_s, s5i_s, conv_s = _layer_stack(
        x_sample, p_sample, state_pool, state_hgrn, state_s5_re, state_s5_im, state_ffn_conv,
        PAST_LEN, lb, wts)
    return (y_p, y_s, pool_p, hg_p, s5r_p, s5i_p, conv_p, pool_s, hg_s, s5r_s, s5i_s, conv_s)
```

```python
import functools

import jax
import jax.numpy as jnp
from jax import lax
from jax.experimental import pallas as pl
from jax.experimental.pallas import tpu as pltpu

D_MODEL = 4096
DEPTH = 4
PAST_LEN = 1024
W_BRANCH = D_MODEL // 2
N_BRANCH = 3
POOL_WINDOWS = (2, 4, 8, 16)
POOL_GROUP = W_BRANCH // len(POOL_WINDOWS)
POOL_HIST = max(POOL_WINDOWS) - 1
HG_DK = 128
HG_HEADS = W_BRANCH // HG_DK
HG_DV = W_BRANCH // HG_HEADS
HG_BLOCK = 16
S5_GROUP = 16
S5_GROUPS = W_BRANCH // S5_GROUP
S5_STATE = 64
D_FF = 11008
CONV_W = 3
PLE_DIM = 256
N_MIX = 6 * W_BRANCH
N_IN = N_MIX + N_BRANCH * D_MODEL
EPS = 1e-6

F32 = jnp.float32
BF16 = jnp.bfloat16

VMEM_LIMIT_BYTES = 56 * 1024 * 1024
SUBLANES = 8
BF16_ROWS = 16

S5_CHUNK_GROUPS = 16
S5_CHUNKS = S5_GROUPS // S5_CHUNK_GROUPS
S5_CH = S5_CHUNK_GROUPS * S5_GROUP
S5_ST = S5_CHUNK_GROUPS * S5_STATE


def _params(*sem, flags=None):
    return pltpu.CompilerParams(dimension_semantics=sem, vmem_limit_bytes=VMEM_LIMIT_BYTES, flags=flags)


def _dot(a, b):
    return jnp.dot(a, b, preferred_element_type=F32)


def _norm_kernel(x_ref, g_ref, o_ref):
    x = x_ref[...]
    ms = jnp.mean(x * x, axis=-1, keepdims=True)
    o_ref[...] = ((x * lax.rsqrt(ms + EPS)) * g_ref[...]).astype(o_ref.dtype)


def rms_norm(x, g, out_dtype):
    T, D = x.shape
    tr = min(T, 512)
    return pl.pallas_call(
        _norm_kernel,
        grid=(T // tr,),
        in_specs=[pl.BlockSpec((tr, D), lambda i: (i, 0)),
                  pl.BlockSpec((1, D), lambda i: (0, 0))],
        out_specs=pl.BlockSpec((tr, D), lambda i: (i, 0)),
        out_shape=jax.ShapeDtypeStruct((T, D), out_dtype),
        compiler_params=_params("parallel"),
        name="rms_norm",
    )(x, g.reshape(1, D))


def _cast_kernel(x_ref, o_ref):
    o_ref[...] = x_ref[...].astype(o_ref.dtype)


def cast_bf16(w):
    shape = w.shape
    N = shape[-1]
    R = w.size // N
    tr = 1024
    tc = min(N, 2048)
    out = pl.pallas_call(
        _cast_kernel,
        grid=(R // tr, pl.cdiv(N, tc)),
        in_specs=[pl.BlockSpec((tr, tc), lambda i, j: (i, j))],
        out_specs=pl.BlockSpec((tr, tc), lambda i, j: (i, j)),
        out_shape=jax.ShapeDtypeStruct((R, N), BF16),
        compiler_params=_params("parallel", "parallel"),
        name="cast_bf16",
    )(w.reshape(R, N))
    return out.reshape(shape)


def _mm_kernel(a_ref, b_ref, o_ref):
    o_ref[...] = _dot(a_ref[...], b_ref[...]).astype(o_ref.dtype)


def matmul(a, b, l, out_dtype, *, tm, tn, col_block0, n_cols, name):
    M, K = a.shape
    return pl.pallas_call(
        _mm_kernel,
        grid=(M // tm, n_cols // tn),
        in_specs=[pl.BlockSpec((tm, K), lambda i, j: (i, 0)),
                  pl.BlockSpec((None, K, tn), lambda i, j: (l, 0, j + col_block0))],
        out_specs=pl.BlockSpec((tm, tn), lambda i, j: (i, j)),
        out_shape=jax.ShapeDtypeStruct((M, n_cols), out_dtype),
        compiler_params=_params("parallel", "arbitrary"),
        name=name,
    )(a, b)


def _mm_res_kernel(a_ref, b_ref, h_ref, o_ref):
    o_ref[...] = h_ref[...] + _dot(a_ref[...], b_ref[...])


def matmul_residual(a, b, l, h, *, tm, tn, name):
    M, K = a.shape
    N = b.shape[-1]
    return pl.pallas_call(
        _mm_res_kernel,
        grid=(M // tm, N // tn),
        in_specs=[pl.BlockSpec((tm, K), lambda i, j: (i, 0)),
                  pl.BlockSpec((None, K, tn), lambda i, j: (l, 0, j)),
                  pl.BlockSpec((tm, tn), lambda i, j: (i, j))],
        out_specs=pl.BlockSpec((tm, tn), lambda i, j: (i, j)),
        out_shape=jax.ShapeDtypeStruct((M, N), F32),
        input_output_aliases={2: 0},
        compiler_params=_params("parallel", "arbitrary"),
        name=name,
    )(a, b, h)


def _lb_kernel(x_ref, o_ref):
    x = x_ref[...]
    m = jnp.max(x, axis=0, keepdims=True)
    e = jnp.exp(x - m)
    p = e / jnp.sum(e, axis=0, keepdims=True)
    rows = [p[0:1]]
    for l in range(1, DEPTH):
        rows.append(rows[-1] + p[l:l + 1])
    o_ref[...] = jnp.concatenate([r - rows[0] for r in rows], axis=0)


def forget_lower_bounds(logits):
    return pl.pallas_call(
        _lb_kernel,
        out_shape=jax.ShapeDtypeStruct(logits.shape, F32),
        name="hgrn_lower_bounds",
    )(logits)


def _pool_kernel(u_ref, prev_ref, hist_ref, w_ref, scale_ref, o_ref, st_ref, ext_ref, *, tt, pos0):
    ti = pl.program_id(1)
    nt = pl.num_programs(1)
    hist_rows = POOL_HIST + 1

    @pl.when(ti == 0)
    def _():
        ext_ref[0:hist_rows, :] = hist_ref[0]

    @pl.when(ti > 0)
    def _():
        ext_ref[0:hist_rows, :] = prev_ref[...]

    cur = u_ref[...]
    ext_ref[hist_rows:, :] = cur
    pos = pos0 + ti * tt + lax.broadcasted_iota(jnp.int32, (tt, 1), 0)
    for gi, w in enumerate(POOL_WINDOWS):
        cols = slice(gi * POOL_GROUP, (gi + 1) * POOL_GROUP)
        win = ext_ref[pl.ds(hist_rows, tt), cols]
        for k in range(1, w):
            win = win + ext_ref[pl.ds(hist_rows - k, tt), cols]
        cnt = jnp.minimum(pos + 1, w).astype(F32)
        pooled = win / cnt - cur[:, cols]
        y = _dot(pooled.astype(BF16), w_ref[gi]) * scale_ref[:, cols]
        o_ref[:, cols] = y.astype(o_ref.dtype)

    @pl.when(ti == nt - 1)
    def _():
        st_ref[0] = ext_ref[pl.ds(tt + 1, POOL_HIST), :]


def pool_mixer(proj, hist, w_bf16, l, scale, *, B, L, pos0):
    W = W_BRANCH
    tt = min(L, 256)
    nt = L // tt
    hist16 = jnp.concatenate([jnp.zeros((B, 1, W), F32), hist], axis=1)
    blk16 = tt // (POOL_HIST + 1)
    kern = functools.partial(_pool_kernel, tt=tt, pos0=pos0)
    return pl.pallas_call(
        kern,
        grid=(B, nt),
        in_specs=[pl.BlockSpec((tt, W), lambda b, t: (b * nt + t, 0)),
                  pl.BlockSpec((POOL_HIST + 1, W),
                               lambda b, t: (jnp.maximum((b * nt + t) * blk16 - 1, 0), 0)),
                  pl.BlockSpec((1, POOL_HIST + 1, W), lambda b, t: (b, 0, 0)),
                  pl.BlockSpec((None, len(POOL_WINDOWS), POOL_GROUP, POOL_GROUP), lambda b, t: (l, 0, 0, 0)),
                  pl.BlockSpec((1, W), lambda b, t: (0, 0))],
        out_specs=[pl.BlockSpec((tt, W), lambda b, t: (b * nt + t, 0)),
                   pl.BlockSpec((1, POOL_HIST, W), lambda b, t: (b, 0, 0))],
        out_shape=[jax.ShapeDtypeStruct((B * L, W), BF16),
                   jax.ShapeDtypeStruct((B, POOL_HIST, W), F32)],
        scratch_shapes=[pltpu.VMEM((tt + POOL_HIST + 1, W), F32)],
        compiler_params=_params("parallel", "arbitrary"),
        name="pool_mixer",
    )(proj, proj, hist16, w_bf16, scale.reshape(1, W))


HG_HEADS_PER_STEP = 16


def _hgrn_kernel(q_ref, f_ref, v_ref, g_ref, lb_ref, ng_ref, s0_ref, o_ref, s_ref, st_ref, oacc_ref, *, C):
    ci = pl.program_id(2)
    nc = pl.num_programs(2)
    hb = HG_HEADS_PER_STEP
    nsub = C // HG_BLOCK
    ref_row = HG_BLOCK // 2 - 1

    @pl.when(ci == 0)
    def _():
        for h in range(hb):
            st_ref[h] = s0_ref[0, h].T

    lb = lb_ref[...]
    f = lb + (1.0 - lb) * jax.nn.sigmoid(f_ref[...])
    logf = jnp.log(f)
    kk = 1.0 - f
    tri = (lax.broadcasted_iota(jnp.int32, (C, C), 1)
           <= lax.broadcasted_iota(jnp.int32, (C, C), 0)).astype(F32)
    bcum = jnp.dot(tri, logf, precision=lax.Precision.HIGHEST, preferred_element_type=F32)
    blast = bcum[C - 1:C, :]
    q = q_ref[...]
    v = v_ref[...]
    qd = (q * jnp.exp(bcum)).astype(BF16)
    kdec = (kk * jnp.exp(blast - bcum)).astype(BF16)
    eb = jnp.exp(blast)
    vb = v.astype(BF16)
    gate = jax.nn.silu(g_ref[...])
    ng = ng_ref[...]
    for h in range(hb):
        sl = slice(h * HG_DK, (h + 1) * HG_DK)
        st = st_ref[h]
        oacc_ref[...] = lax.dot_general(qd[:, sl], st.astype(BF16), (((1,), (1,)), ((), ())),
                                        preferred_element_type=F32)
        for j in range(nsub):
            r0 = j * HG_BLOCK
            bj = bcum[r0:, sl]
            bref = bcum[r0 + ref_row:r0 + ref_row + 1, sl]
            qj = (q[r0:, sl] * jnp.exp(bj - bref)).astype(BF16)
            kj = (kk[r0:r0 + HG_BLOCK, sl] * jnp.exp(bref - bj[:HG_BLOCK])).astype(BF16)
            att = lax.dot_general(qj, kj, (((1,), (1,)), ((), ())), preferred_element_type=F32)
            rr = lax.broadcasted_iota(jnp.int32, att.shape, 0)
            cc = lax.broadcasted_iota(jnp.int32, att.shape, 1)
            att = jnp.where(cc <= rr, att, 0.0)
            oacc_ref[r0:, :] += _dot(att.astype(BF16), vb[r0:r0 + HG_BLOCK, sl])
        o = oacc_ref[...]
        o = o * lax.rsqrt(jnp.mean(o * o, axis=-1, keepdims=True) + EPS)
        o = o * ng[:, sl]
        o_ref[:, sl] = (o * gate[:, sl]).astype(o_ref.dtype)
        upd = lax.dot_general(vb[:, sl], kdec[:, sl], (((0,), (0,)), ((), ())),
                              preferred_element_type=F32)
        st_ref[h] = eb[:, sl] * st + upd

    @pl.when(ci == nc - 1)
    def _():
        for h in range(hb):
            s_ref[0, h] = st_ref[h].T


def hgrn2_mixer(proj, lb, norm_g, s0, *, B, L):
    W = W_BRANCH
    C = min(L, 128)
    nc = L // C
    hb = HG_HEADS_PER_STEP
    nh = HG_HEADS // hb
    wb = hb * HG_DK
    per = W // wb

    def col(k):
        return pl.BlockSpec((C, wb), lambda b, h, c: (b * nc + c, k * per + h))

    vec = pl.BlockSpec((1, wb), lambda b, h, c: (0, h))
    st = pl.BlockSpec((1, hb, HG_DK, HG_DV), lambda b, h, c: (b, h, 0, 0))
    kern = functools.partial(_hgrn_kernel, C=C)
    return pl.pallas_call(
        kern,
        grid=(B, nh, nc),
        in_specs=[col(1), col(2), col(3), col(4), vec, vec, st],
        out_specs=[pl.BlockSpec((C, wb), lambda b, h, c: (b * nc + c, h)), st],
        out_shape=[jax.ShapeDtypeStruct((B * L, W), BF16),
                   jax.ShapeDtypeStruct((B, HG_HEADS, HG_DK, HG_DV), F32)],
        scratch_shapes=[pltpu.VMEM((hb, HG_DV, HG_DK), F32),
                        pltpu.VMEM((C, HG_DV), F32)],
        compiler_params=_params("parallel", "parallel", "arbitrary"),
        name="hgrn2_mixer",
    )(proj, proj, proj, proj, lb.reshape(1, W), norm_g.reshape(1, W), s0)


S5_CHUNKS_PER_STEP = 4


def _s5_scan_tile(xr, xi, cr, cim, pw_ref, cc):
    for idx, k in enumerate((1, 2, 4)):
        sr = pltpu.roll(xr, k, 0)
        si = pltpu.roll(xi, k, 0)
        mr = pw_ref[cc, 2 * idx]
        mi = pw_ref[cc, 2 * idx + 1]
        xr, xi = xr + (mr * sr - mi * si), xi + (mr * si + mi * sr)
    pr = pw_ref[cc, 6]
    pim = pw_ref[cc, 7]
    return xr + (pr * cr - pim * cim), xi + (pr * cim + pim * cr)


def _s5_kernel(u_ref, bw_ref, cw_ref, pw_ref, x0_ref, d_ref, y_ref, xo_ref,
               xr0_s, xi0_s, xr1_s, xi1_s, car_s, *, nseq, seg, nt):
    ti = pl.program_id(2)
    bufs = ((xr0_s, xi0_s), (xr1_s, xi1_s))

    def bcast(row):
        return jnp.broadcast_to(row, (SUBLANES, S5_ST))

    if nt > 1:
        @pl.when(ti == 0)
        def _():
            for cc in range(S5_CHUNKS_PER_STEP):
                car_s[cc, 0] = bcast(x0_ref[0, cc, 0:1, :])
                car_s[cc, 1] = bcast(x0_ref[0, cc, 1:2, :])

    for cc in range(S5_CHUNKS_PER_STEP):
        xr_s, xi_s = bufs[cc % 2]
        lanes = slice(cc * S5_CH, (cc + 1) * S5_CH)
        u = u_ref[:, lanes]
        bu = _dot(u.astype(BF16), bw_ref[cc])
        xr_s[...] = bu[:, :S5_ST]
        xi_s[...] = bu[:, S5_ST:]
        for q in range(nseq):
            if nt > 1:
                cr, cim = car_s[cc, 0], car_s[cc, 1]
            else:
                cr, cim = bcast(x0_ref[q, cc, 0:1, :]), bcast(x0_ref[q, cc, 1:2, :])
            for r in range(seg // SUBLANES):
                rows = slice(q * seg + r * SUBLANES, q * seg + (r + 1) * SUBLANES)
                xr, xi = _s5_scan_tile(xr_s[rows, :], xi_s[rows, :], cr, cim, pw_ref, cc)
                xr_s[rows, :] = xr
                xi_s[rows, :] = xi
                cr, cim = bcast(xr[SUBLANES - 1:SUBLANES]), bcast(xi[SUBLANES - 1:SUBLANES])
            if nt > 1:
                car_s[cc, 0] = cr
                car_s[cc, 1] = cim
            else:
                xo_ref[q, cc] = jnp.concatenate([cr[0:1], cim[0:1]], axis=0)
        y = _dot(xr_s[...].astype(BF16), cw_ref[cc, 0]) + _dot(xi_s[...].astype(BF16), cw_ref[cc, 1])
        y = y + d_ref[:, lanes] * u
        y_ref[:, lanes] = jax.nn.gelu(y).astype(y_ref.dtype)

    if nt > 1:
        @pl.when(ti == nt - 1)
        def _():
            for cc in range(S5_CHUNKS_PER_STEP):
                xo_ref[0, cc] = jnp.concatenate([car_s[cc, 0, 0:1, :], car_s[cc, 1, 0:1, :]], axis=0)


def _s5_discretize(a_re, a_im, log_step, b_re, b_im, c_re, c_im):
    dt = jnp.exp(log_step)[:, None]
    mag = jnp.exp(dt * a_re)
    ab_re = mag * jnp.cos(dt * a_im)
    ab_im = mag * jnp.sin(dt * a_im)
    den = a_re * a_re + a_im * a_im
    coef_re = ((ab_re - 1.0) * a_re + ab_im * a_im) / den
    coef_im = (ab_im * a_re - (ab_re - 1.0) * a_im) / den
    bb_re = coef_re[..., None] * b_re - coef_im[..., None] * b_im
    bb_im = coef_re[..., None] * b_im + coef_im[..., None] * b_re
    eye = jnp.eye(S5_CHUNK_GROUPS, dtype=F32)

    def blockdiag_b(bb):
        t = bb.reshape(S5_CHUNKS, S5_CHUNK_GROUPS, S5_STATE, S5_GROUP).transpose(0, 1, 3, 2)
        return jnp.einsum('cgxp,gh->cgxhp', t, eye).reshape(S5_CHUNKS, S5_CH, S5_ST)

    def blockdiag_c(cc):
        t = cc.reshape(S5_CHUNKS, S5_CHUNK_GROUPS, S5_GROUP, S5_STATE)
        return jnp.einsum('cgxp,gh->cgphx', t, eye).reshape(S5_CHUNKS, S5_ST, S5_CH)

    bw = jnp.concatenate([blockdiag_b(bb_re), blockdiag_b(bb_im)], axis=2).astype(BF16)
    cw = jnp.stack([blockdiag_c(c_re), -blockdiag_c(c_im)], axis=1).astype(BF16)

    pows = [(ab_re, ab_im)]
    for _ in range(SUBLANES - 1):
        pr, pi = pows[-1]
        pows.append((pr * ab_re - pi * ab_im, pr * ab_im + pi * ab_re))

    def plane(vals):
        return jnp.stack([v.reshape(S5_CHUNKS, S5_ST) for v in vals], axis=1)

    zero = jnp.zeros_like(ab_re)
    planes = []
    for k in (1, 2, 4):
        planes.append(plane([pows[k - 1][0] if t >= k else zero for t in range(SUBLANES)]))
        planes.append(plane([pows[k - 1][1] if t >= k else zero for t in range(SUBLANES)]))
    planes.append(plane([pows[t][0] for t in range(SUBLANES)]))
    planes.append(plane([pows[t][1] for t in range(SUBLANES)]))
    pw = jnp.stack(planes, axis=1)
    return bw, cw, pw


def s5_mixer(proj, x0_re, x0_im, disc, d_skip, *, B, L):
    W = W_BRANCH
    bw, cw, pw = disc
    cps = S5_CHUNKS_PER_STEP
    if L >= 512:
        nseq, seg, nt, bsteps = 1, 512, L // 512, B
    else:
        nseq, seg, nt, bsteps = B, L, 1, 1
    rows = nseq * seg
    col0 = 5 * W // (cps * S5_CH)
    x0 = jnp.stack([x0_re.reshape(B, S5_CHUNKS, S5_ST), x0_im.reshape(B, S5_CHUNKS, S5_ST)], axis=2)
    kern = functools.partial(_s5_kernel, nseq=nseq, seg=seg, nt=nt)
    state = pl.BlockSpec((nseq, cps, 2, S5_ST), lambda b, c, t: (b, c, 0, 0))
    y, xo = pl.pallas_call(
        kern,
        grid=(bsteps, S5_CHUNKS // cps, nt),
        in_specs=[pl.BlockSpec((rows, cps * S5_CH), lambda b, c, t: (b * nt + t, col0 + c)),
                  pl.BlockSpec((cps, S5_CH, 2 * S5_ST), lambda b, c, t: (c, 0, 0)),
                  pl.BlockSpec((cps, 2, S5_ST, S5_CH), lambda b, c, t: (c, 0, 0, 0)),
                  pl.BlockSpec((cps, 8, SUBLANES, S5_ST), lambda b, c, t: (c, 0, 0, 0)),
                  state,
                  pl.BlockSpec((1, cps * S5_CH), lambda b, c, t: (0, c))],
        out_specs=[pl.BlockSpec((rows, cps * S5_CH), lambda b, c, t: (b * nt + t, c)), state],
        out_shape=[jax.ShapeDtypeStruct((B * L, W), BF16),
                   jax.ShapeDtypeStruct((B, S5_CHUNKS, 2, S5_ST), F32)],
        scratch_shapes=[pltpu.VMEM((rows, S5_ST), F32), pltpu.VMEM((rows, S5_ST), F32),
                        pltpu.VMEM((rows, S5_ST), F32), pltpu.VMEM((rows, S5_ST), F32),
                        pltpu.VMEM((cps, 2, SUBLANES, S5_ST), F32)],
        compiler_params=_params("parallel", "parallel", "arbitrary"),
        name="s5_scan",
    )(proj, bw, cw, pw, x0, d_skip.reshape(1, W))
    s_re = xo[:, :, 0, :].reshape(B, S5_GROUPS, S5_STATE)
    s_im = xo[:, :, 1, :].reshape(B, S5_GROUPS, S5_STATE)
    return y, s_re, s_im


def _glu_kernel(a_ref, b_ref, y_ref, bias_ref, o_ref):
    z = _dot(a_ref[...], b_ref[...]) + bias_ref[...]
    o_ref[...] = (y_ref[...].astype(F32) * jax.nn.sigmoid(z)).astype(o_ref.dtype)


def s5_glu(y, w_bf16, l, bias, *, tm, tn):
    M, K = y.shape
    N = w_bf16.shape[-1]
    return pl.pallas_call(
        _glu_kernel,
        grid=(M // tm, N // tn),
        in_specs=[pl.BlockSpec((tm, K), lambda i, j: (i, 0)),
                  pl.BlockSpec((None, K, tn), lambda i, j: (l, 0, j)),
                  pl.BlockSpec((tm, tn), lambda i, j: (i, j)),
                  pl.BlockSpec((1, tn), lambda i, j: (0, j))],
        out_specs=pl.BlockSpec((tm, tn), lambda i, j: (i, j)),
        out_shape=jax.ShapeDtypeStruct((M, N), BF16),
        compiler_params=_params("parallel", "arbitrary"),
        name="s5_glu",
    )(y, w_bf16, y, bias.reshape(1, N))


def _merge_kernel(oa_ref, ob_ref, oc_ref, w_ref, ga_ref, gb_ref, gc_ref, o_ref):
    acc = jax.nn.sigmoid(ga_ref[...].astype(F32)) * _dot(oa_ref[...], w_ref[0])
    acc = acc + jax.nn.sigmoid(gb_ref[...].astype(F32)) * _dot(ob_ref[...], w_ref[1])
    acc = acc + jax.nn.sigmoid(gc_ref[...].astype(F32)) * _dot(oc_ref[...], w_ref[2])
    o_ref[...] = acc.astype(o_ref.dtype)


def branch_merge(o_a, o_b, o_c, w_br_bf16, l, gates, *, tm, tn):
    M, W = o_a.shape
    D = D_MODEL
    nj = D // tn
    act = pl.BlockSpec((tm, W), lambda i, j: (i, 0))

    def gate(n):
        return pl.BlockSpec((tm, tn), lambda i, j: (i, n * nj + j))

    return pl.pallas_call(
        _merge_kernel,
        grid=(M // tm, nj),
        in_specs=[act, act, act,
                  pl.BlockSpec((None, N_BRANCH, W, tn), lambda i, j: (l, 0, 0, j)),
                  gate(0), gate(1), gate(2)],
        out_specs=pl.BlockSpec((tm, tn), lambda i, j: (i, j)),
        out_shape=jax.ShapeDtypeStruct((M, D), BF16),
        compiler_params=_params("parallel", "arbitrary"),
        name="branch_merge",
    )(o_a, o_b, o_c, w_br_bf16, gates, gates, gates)


def _conv_gate(ua0, ua1, ua2, uv0, uv1, uv2, cwa_ref, cwv_ref, cba_ref, cbv_ref):
    shape = ua0.shape
    tn = shape[-1]

    def tiles(x):
        return x.reshape(-1, SUBLANES, tn)

    def tap(ref, k):
        return ref[k * SUBLANES:(k + 1) * SUBLANES, :][None]

    a = cba_ref[...][None] + (tap(cwa_ref, 0) * tiles(ua0) + tap(cwa_ref, 1) * tiles(ua1) + tap(cwa_ref, 2) * tiles(ua2))
    v = cbv_ref[...][None] + (tap(cwv_ref, 0) * tiles(uv0) + tap(cwv_ref, 1) * tiles(uv1) + tap(cwv_ref, 2) * tiles(uv2))
    return (jax.nn.gelu(a) * v).reshape(shape)


def _up_conv_long_kernel(x_ref, xp_ref, wal_ref, wah_ref, wvl_ref, wvh_ref, cwa_ref, cwv_ref, cba_ref, cbv_ref,
                         hist_ref, g_ref, tail_ref, xe_s, *, tm, tiles_per_seq):
    i = pl.program_id(0)
    j = pl.program_id(1)
    pad = BF16_ROWS
    half = wal_ref.shape[1]

    @pl.when(j == 0)
    def _():
        xe_s[0:pad, :] = xp_ref[...]
        xe_s[pad:, :] = x_ref[...]

    xe = xe_s[...]
    seq_start = (i % tiles_per_seq) == 0
    use_hist = (lax.broadcasted_iota(jnp.int32, (pad, half), 0) >= pad - 2) & seq_start

    def with_history(u, hist_rows):
        return jnp.concatenate([jnp.where(use_hist, hist_rows, u[:pad]), u[pad:]], axis=0)

    def taps(u):
        return pltpu.roll(u, 2, 0)[pad:], pltpu.roll(u, 1, 0)[pad:], u[pad:]

    for c, (wa_ref, wv_ref) in enumerate(((wal_ref, wvl_ref), (wah_ref, wvh_ref))):
        cols = slice(c * half, (c + 1) * half)
        ua = with_history(_dot(xe, wa_ref[...]), hist_ref[0, 0, :, cols])
        uv = with_history(_dot(xe, wv_ref[...]), hist_ref[1, 0, :, cols])
        g = _conv_gate(*taps(ua), *taps(uv), cwa_ref[:, cols], cwv_ref[:, cols], cba_ref[:, cols], cbv_ref[:, cols])
        g_ref[:, cols] = g.astype(g_ref.dtype)
        tail_ref[0, 0, :, cols] = ua[pad + tm - 2:]
        tail_ref[1, 0, :, cols] = uv[pad + tm - 2:]


def _up_conv_short_kernel(x_ref, wal_ref, wah_ref, wvl_ref, wvh_ref, cwa_ref, cwv_ref, cba_ref, cbv_ref, hist_ref,
                          g_ref, tail_ref, ua_s, uv_s, *, nseq, L):
    x = x_ref[...]
    half = wal_ref.shape[1]
    tn = 2 * half
    pad = SUBLANES
    ua_s[:, pad:, :half] = _dot(x, wal_ref[...]).reshape(nseq, L, half)
    ua_s[:, pad:, half:] = _dot(x, wah_ref[...]).reshape(nseq, L, half)
    uv_s[:, pad:, :half] = _dot(x, wvl_ref[...]).reshape(nseq, L, half)
    uv_s[:, pad:, half:] = _dot(x, wvh_ref[...]).reshape(nseq, L, half)
    ua_s[:, pad - 2:pad, :] = hist_ref[0]
    uv_s[:, pad - 2:pad, :] = hist_ref[1]
    g = _conv_gate(ua_s[:, pad - 2:pad - 2 + L, :], ua_s[:, pad - 1:pad - 1 + L, :], ua_s[:, pad:pad + L, :],
                   uv_s[:, pad - 2:pad - 2 + L, :], uv_s[:, pad - 1:pad - 1 + L, :], uv_s[:, pad:pad + L, :],
                   cwa_ref, cwv_ref, cba_ref, cbv_ref)
    g_ref[...] = g.reshape(nseq * L, tn).astype(g_ref.dtype)
    tail_ref[0] = ua_s[:, pad + L - 2:pad + L, :]
    tail_ref[1] = uv_s[:, pad + L - 2:pad + L, :]


def conv_ffn_up(xn, w_up, l, conv_w, conv_b, hist, *, B, L):
    T, D = xn.shape
    tn = 512
    half = tn // 2
    nj = pl.cdiv(D_FF, tn)
    voff = D_FF // half
    last = 2 * D_FF // half - 1
    cw8 = jnp.repeat(conv_w, SUBLANES, axis=0)
    cb8 = jnp.broadcast_to(conv_b[None, :], (SUBLANES, 2 * D_FF))
    cwa, cwv = cw8[:, :D_FF], cw8[:, D_FF:]
    cba, cbv = cb8[:, :D_FF], cb8[:, D_FF:]
    hist2 = jnp.stack([hist[:, :, :D_FF], hist[:, :, D_FF:]], axis=0)
    if L >= 1024:
        tm = 1024
        tps = L // tm
        nti = T // tm

        def wspec(k):
            return pl.BlockSpec((None, D, half), lambda i, j: (l, 0, jnp.minimum(2 * j + k, last)))

        cwspec = pl.BlockSpec((CONV_W * SUBLANES, tn), lambda i, j: (0, j))
        cbspec = pl.BlockSpec((SUBLANES, tn), lambda i, j: (0, j))
        kern = functools.partial(_up_conv_long_kernel, tm=tm, tiles_per_seq=tps)
        g, tails = pl.pallas_call(
            kern,
            grid=(nti, nj),
            in_specs=[pl.BlockSpec((tm, D), lambda i, j: (i, 0)),
                      pl.BlockSpec((BF16_ROWS, D),
                                   lambda i, j: (jnp.maximum(i * (tm // BF16_ROWS) - 1, 0), 0)),
                      wspec(0), wspec(1), wspec(voff), wspec(voff + 1),
                      cwspec, cwspec, cbspec, cbspec,
                      pl.BlockSpec((2, 1, BF16_ROWS, tn), lambda i, j: (0, i // tps, 0, j))],
            out_specs=[pl.BlockSpec((tm, tn), lambda i, j: (i, j)),
                       pl.BlockSpec((2, 1, 2, tn), lambda i, j: (0, i, 0, j))],
            out_shape=[jax.ShapeDtypeStruct((T, D_FF), BF16),
                       jax.ShapeDtypeStruct((2, nti, 2, D_FF), F32)],
            scratch_shapes=[pltpu.VMEM((tm + BF16_ROWS, D), BF16)],
            compiler_params=_params("parallel", "arbitrary"),
            name="conv_ffn_up_long",
        )(xn, xn, w_up, w_up, w_up, w_up, cwa, cwv, cba, cbv,
          jnp.pad(hist2, ((0, 0), (0, 0), (BF16_ROWS - 2, 0), (0, 0))))
        tails = tails[:, tps - 1::tps]
    else:
        def wspec(k):
            return pl.BlockSpec((None, D, half), lambda j: (l, 0, jnp.minimum(2 * j + k, last)))

        kern = functools.partial(_up_conv_short_kernel, nseq=B, L=L)
        g, tails = pl.pallas_call(
            kern,
            grid=(nj,),
            in_specs=[pl.BlockSpec((T, D), lambda j: (0, 0)),
                      wspec(0), wspec(1), wspec(voff), wspec(voff + 1),
                      pl.BlockSpec((CONV_W * SUBLANES, tn), lambda j: (0, j)),
                      pl.BlockSpec((CONV_W * SUBLANES, tn), lambda j: (0, j)),
                      pl.BlockSpec((SUBLANES, tn), lambda j: (0, j)),
                      pl.BlockSpec((SUBLANES, tn), lambda j: (0, j)),
                      pl.BlockSpec((2, B, 2, tn), lambda j: (0, 0, 0, j))],
            out_specs=[pl.BlockSpec((T, tn), lambda j: (0, j)),
                       pl.BlockSpec((2, B, 2, tn), lambda j: (0, 0, 0, j))],
            out_shape=[jax.ShapeDtypeStruct((T, D_FF), BF16),
                       jax.ShapeDtypeStruct((2, B, 2, D_FF), F32)],
            scratch_shapes=[pltpu.VMEM((B, L + SUBLANES, tn), F32),
                            pltpu.VMEM((B, L + SUBLANES, tn), F32)],
            compiler_params=_params("arbitrary"),
            name="conv_ffn_up_short",
        )(xn, w_up, w_up, w_up, w_up, cwa, cwv, cba, cbv, hist2)
    s_conv = jnp.concatenate([tails[0], tails[1]], axis=-1)
    return g, s_conv


def _ple_kernel(x_ref, wg_ref, p_ref, wp_ref, h_ref, o_ref):
    gate = jax.nn.sigmoid(_dot(x_ref[...], wg_ref[...]))
    emb = _dot(p_ref[...].astype(BF16), wp_ref[...])
    o_ref[...] = h_ref[...] + gate * emb


def ple_update(xn, w_pg, p, w_ple, l, h, *, tm, tn):
    M, D = xn.shape
    return pl.pallas_call(
        _ple_kernel,
        grid=(M // tm, D // tn),
        in_specs=[pl.BlockSpec((tm, D), lambda i, j: (i, 0)),
                  pl.BlockSpec((None, D, tn), lambda i, j: (l, 0, j)),
                  pl.BlockSpec((tm, PLE_DIM), lambda i, j: (i, 0)),
                  pl.BlockSpec((None, PLE_DIM, tn), lambda i, j: (l, 0, j)),
                  pl.BlockSpec((tm, tn), lambda i, j: (i, j))],
        out_specs=pl.BlockSpec((tm, tn), lambda i, j: (i, j)),
        out_shape=jax.ShapeDtypeStruct((M, D), F32),
        input_output_aliases={4: 0},
        compiler_params=_params("parallel", "arbitrary"),
        name="ple_update",
    )(xn, w_pg, p, w_ple, h)


def _layer_stack(x, p, st_pool, st_hg, st_s5r, st_s5i, st_conv, pos0, lb, wts):
    B, L, D = x.shape
    T = B * L
    W = W_BRANCH
    tm = min(T, 1024)
    h = x.reshape(T, D)
    n_pool, n_hg, n_s5r, n_s5i, n_conv = [], [], [], [], []
    w = wts
    for l in range(DEPTH):
        xn = rms_norm(h, w['norm_mix'][l], BF16)
        proj = matmul(xn, w['w_in'], l, F32, tm=tm, tn=1024, col_block0=0, n_cols=N_MIX, name="in_proj_mix")
        gates = matmul(xn, w['w_in'], l, BF16, tm=tm, tn=1024, col_block0=N_MIX // 1024,
                       n_cols=N_BRANCH * D, name="in_proj_gates")
        o_a, s_pool = pool_mixer(proj, st_pool[l], w['pool_w'], l, w['pool_scale'][l], B=B, L=L, pos0=pos0)
        o_b, s_hg = hgrn2_mixer(proj, lb[l], w['hg_norm'][l], st_hg[l], B=B, L=L)
        y_c, s_r, s_i = s5_mixer(proj, st_s5r[l], st_s5i[l], w['s5_disc'][l], w['s5_d'][l], B=B, L=L)
        o_c = s5_glu(y_c, w['s5_w_glu'], l, w['s5_b_glu'][l], tm=tm, tn=1024)
        merged = branch_merge(o_a, o_b, o_c, w['w_br'], l, gates, tm=tm, tn=512)
        h = matmul_residual(merged, w['w_out'], l, h, tm=tm, tn=1024, name="out_proj")
        xn = rms_norm(h, w['norm_ffn'][l], BF16)
        g, s_conv = conv_ffn_up(xn, w['w_up'], l, w['conv_w'][l], w['conv_b'][l], st_conv[l], B=B, L=L)
        h = matmul_residual(g, w['w_down'], l, h, tm=min(T, 512), tn=512, name="ffn_down")
        xn = rms_norm(h, w['norm_ple'][l], BF16)
        h = ple_update(xn, w['w_ple_gate'], p[l].reshape(T, PLE_DIM), w['w_ple'], l, h, tm=tm, tn=512)
        n_pool.append(s_pool)
        n_hg.append(s_hg)
        n_s5r.append(s_r)
        n_s5i.append(s_i)
        n_conv.append(s_conv)
    y = rms_norm(h, w['norm_final'], F32).reshape(B, L, D)
    return (y, jnp.stack(n_pool), jnp.stack(n_hg), jnp.stack(n_s5r), jnp.stack(n_s5i), jnp.stack(n_conv))


def kernel(x_prompt, x_sample, state_pool, state_hgrn, state_s5_re, state_s5_im, state_ffn_conv,
           p_prompt, p_sample, norm_mix, w_in, pool_w, pool_scale, hg_lb_logits, hg_norm,
           s5_a_re, s5_a_im, s5_log_step, s5_b_re, s5_b_im, s5_c_re, s5_c_im, s5_d, s5_w_glu, s5_b_glu,
           w_br, w_out, norm_ffn, w_up, conv_w, conv_b, w_down, norm_ple, w_ple_gate, w_ple, norm_final):
    lb = forget_lower_bounds(hg_lb_logits)
    wts = {
        'norm_mix': norm_mix, 'w_in': cast_bf16(w_in), 'pool_w': cast_bf16(pool_w),
        'pool_scale': pool_scale, 'hg_norm': hg_norm,
        's5_disc': [_s5_discretize(s5_a_re[l], s5_a_im[l], s5_log_step[l], s5_b_re[l], s5_b_im[l],
                                   s5_c_re[l], s5_c_im[l]) for l in range(DEPTH)],
        's5_d': s5_d, 's5_w_glu': cast_bf16(s5_w_glu), 's5_b_glu': s5_b_glu,
        'w_br': cast_bf16(w_br), 'w_out': cast_bf16(w_out), 'norm_ffn': norm_ffn,
        'w_up': cast_bf16(w_up), 'conv_w': conv_w, 'conv_b': conv_b, 'w_down': cast_bf16(w_down),
        'norm_ple': norm_ple, 'w_ple_gate': cast_bf16(w_ple_gate), 'w_ple': cast_bf16(w_ple),
        'norm_final': norm_final,
    }
    bp = x_prompt.shape[0]
    z_pool = jnp.zeros((DEPTH, bp) + state_pool.shape[2:], F32)
    z_hg = jnp.zeros((DEPTH, bp) + state_hgrn.shape[2:], F32)
    z_s5 = jnp.zeros((DEPTH, bp) + state_s5_re.shape[2:], F32)
    z_conv = jnp.zeros((DEPTH, bp) + state_ffn_conv.shape[2:], F32)
    y_p, pool_p, hg_p, s5r_p, s5i_p, conv_p = _layer_stack(
        x_prompt, p_prompt, z_pool, z_hg, z_s5, z_s5, z_conv, 0, lb, wts)
    y_s, pool_s, hg_s, s5r_s, s5i_s, conv_s = _layer_stack(
        x_sample, p_sample, state_pool, state_hgrn, state_s5_re, state_s5_im, state_ffn_conv,
        PAST_LEN, lb, wts)
    return (y_p, y_s, pool_p, hg_p, s5r_p, s5i_p, conv_p, pool_s, hg_s, s5r_s, s5i_s, conv_s)
```

```python
import functools

import jax
import jax.numpy as jnp
from jax import lax
from jax.experimental import pallas as pl
from jax.experimental.pallas import tpu as pltpu

D_MODEL = 4096
DEPTH = 4
PAST_LEN = 1024
W_BRANCH = D_MODEL // 2
N_BRANCH = 3
POOL_WINDOWS = (2, 4, 8, 16)
POOL_GROUP = W_BRANCH // len(POOL_WINDOWS)
POOL_HIST = max(POOL_WINDOWS) - 1
HG_DK = 128
HG_HEADS = W_BRANCH // HG_DK
HG_DV = W_BRANCH // HG_HEADS
HG_BLOCK = 16
S5_GROUP = 16
S5_GROUPS = W_BRANCH // S5_GROUP
S5_STATE = 64
D_FF = 11008
CONV_W = 3
PLE_DIM = 256
N_MIX = 6 * W_BRANCH
N_IN = N_MIX + N_BRANCH * D_MODEL
EPS = 1e-6

F32 = jnp.float32
BF16 = jnp.bfloat16

VMEM_LIMIT_BYTES = 56 * 1024 * 1024
SUBLANES = 8
BF16_ROWS = 16

S5_CHUNK_GROUPS = 16
S5_CHUNKS = S5_GROUPS // S5_CHUNK_GROUPS
S5_CH = S5_CHUNK_GROUPS * S5_GROUP
S5_ST = S5_CHUNK_GROUPS * S5_STATE


def _params(*sem, flags=None):
    return pltpu.CompilerParams(dimension_semantics=sem, vmem_limit_bytes=VMEM_LIMIT_BYTES, flags=flags)


def _dot(a, b):
    return jnp.dot(a, b, preferred_element_type=F32)


MXU_COLUMNS = 256


def _column_pieces(n, width=MXU_COLUMNS):
    return [slice(c, c + width) for c in range(0, n, width)]


def _norm_kernel(x_ref, g_ref, o_ref):
    x = x_ref[...]
    ms = jnp.mean(x * x, axis=-1, keepdims=True)
    o_ref[...] = ((x * lax.rsqrt(ms + EPS)) * g_ref[...]).astype(o_ref.dtype)


def rms_norm(x, g, out_dtype):
    T, D = x.shape
    tr = min(T, 512)
    return pl.pallas_call(
        _norm_kernel,
        grid=(T // tr,),
        in_specs=[pl.BlockSpec((tr, D), lambda i: (i, 0)),
                  pl.BlockSpec((1, D), lambda i: (0, 0))],
        out_specs=pl.BlockSpec((tr, D), lambda i: (i, 0)),
        out_shape=jax.ShapeDtypeStruct((T, D), out_dtype),
        compiler_params=_params("parallel"),
        name="rms_norm",
    )(x, g.reshape(1, D))


def _cast_kernel(x_ref, o_ref):
    o_ref[...] = x_ref[...].astype(o_ref.dtype)


def cast_bf16(w):
    shape = w.shape
    N = shape[-1]
    R = w.size // N
    tr = 1024
    tc = min(N, 2048)
    out = pl.pallas_call(
        _cast_kernel,
        grid=(R // tr, pl.cdiv(N, tc)),
        in_specs=[pl.BlockSpec((tr, tc), lambda i, j: (i, j))],
        out_specs=pl.BlockSpec((tr, tc), lambda i, j: (i, j)),
        out_shape=jax.ShapeDtypeStruct((R, N), BF16),
        compiler_params=_params("parallel", "parallel"),
        name="cast_bf16",
    )(w.reshape(R, N))
    return out.reshape(shape)


def _mm_kernel(a_ref, b_ref, o_ref):
    o_ref[...] = _dot(a_ref[...], b_ref[...]).astype(o_ref.dtype)


def matmul(a, b, l, out_dtype, *, tm, tn, col_block0, n_cols, name):
    M, K = a.shape
    return pl.pallas_call(
        _mm_kernel,
        grid=(M // tm, n_cols // tn),
        in_specs=[pl.BlockSpec((tm, K), lambda i, j: (i, 0)),
                  pl.BlockSpec((None, K, tn), lambda i, j: (l, 0, j + col_block0))],
        out_specs=pl.BlockSpec((tm, tn), lambda i, j: (i, j)),
        out_shape=jax.ShapeDtypeStruct((M, n_cols), out_dtype),
        compiler_params=_params("parallel", "arbitrary"),
        name=name,
    )(a, b)


def _mm_res_kernel(a_ref, b_ref, h_ref, o_ref):
    o_ref[...] = h_ref[...] + _dot(a_ref[...], b_ref[...])


def matmul_residual(a, b, l, h, *, tm, tn, name, in_place=True):
    M, K = a.shape
    N = b.shape[-1]
    return pl.pallas_call(
        _mm_res_kernel,
        grid=(M // tm, N // tn),
        in_specs=[pl.BlockSpec((tm, K), lambda i, j: (i, 0)),
                  pl.BlockSpec((None, K, tn), lambda i, j: (l, 0, j)),
                  pl.BlockSpec((tm, tn), lambda i, j: (i, j))],
        out_specs=pl.BlockSpec((tm, tn), lambda i, j: (i, j)),
        out_shape=jax.ShapeDtypeStruct((M, N), F32),
        input_output_aliases={2: 0} if in_place else {},
        compiler_params=_params("parallel", "arbitrary"),
        name=name,
    )(a, b, h)


def _lb_kernel(x_ref, o_ref):
    x = x_ref[...]
    m = jnp.max(x, axis=0, keepdims=True)
    e = jnp.exp(x - m)
    p = e / jnp.sum(e, axis=0, keepdims=True)
    rows = [p[0:1]]
    for l in range(1, DEPTH):
        rows.append(rows[-1] + p[l:l + 1])
    o_ref[...] = jnp.concatenate([r - rows[0] for r in rows], axis=0)


def forget_lower_bounds(logits):
    return pl.pallas_call(
        _lb_kernel,
        out_shape=jax.ShapeDtypeStruct(logits.shape, F32),
        name="hgrn_lower_bounds",
    )(logits)


def _pool_kernel(u_ref, prev_ref, hist_ref, w_ref, scale_ref, o_ref, st_ref, ext_ref, *, tt, pos0):
    ti = pl.program_id(1)
    nt = pl.num_programs(1)
    hist_rows = POOL_HIST + 1

    @pl.when(ti == 0)
    def _():
        ext_ref[0:hist_rows, :] = hist_ref[0]

    @pl.when(ti > 0)
    def _():
        ext_ref[0:hist_rows, :] = prev_ref[...]

    cur = u_ref[...]
    ext_ref[hist_rows:, :] = cur
    pos = pos0 + ti * tt + lax.broadcasted_iota(jnp.int32, (tt, 1), 0)
    for gi, w in enumerate(POOL_WINDOWS):
        cols = slice(gi * POOL_GROUP, (gi + 1) * POOL_GROUP)
        win = ext_ref[pl.ds(hist_rows, tt), cols]
        for k in range(1, w):
            win = win + ext_ref[pl.ds(hist_rows - k, tt), cols]
        cnt = jnp.minimum(pos + 1, w).astype(F32)
        pooled = win / cnt - cur[:, cols]
        y = _dot(pooled.astype(BF16), w_ref[gi]) * scale_ref[:, cols]
        o_ref[:, cols] = y.astype(o_ref.dtype)

    @pl.when(ti == nt - 1)
    def _():
        st_ref[0] = ext_ref[pl.ds(tt + 1, POOL_HIST), :]


def pool_mixer(proj, hist, w_bf16, l, scale, *, B, L, pos0):
    W = W_BRANCH
    tt = min(L, 256)
    nt = L // tt
    hist16 = jnp.concatenate([jnp.zeros((B, 1, W), F32), hist], axis=1)
    blk16 = tt // (POOL_HIST + 1)
    kern = functools.partial(_pool_kernel, tt=tt, pos0=pos0)
    return pl.pallas_call(
        kern,
        grid=(B, nt),
        in_specs=[pl.BlockSpec((tt, W), lambda b, t: (b * nt + t, 0)),
                  pl.BlockSpec((POOL_HIST + 1, W),
                               lambda b, t: (jnp.maximum((b * nt + t) * blk16 - 1, 0), 0)),
                  pl.BlockSpec((1, POOL_HIST + 1, W), lambda b, t: (b, 0, 0)),
                  pl.BlockSpec((None, len(POOL_WINDOWS), POOL_GROUP, POOL_GROUP), lambda b, t: (l, 0, 0, 0)),
                  pl.BlockSpec((1, W), lambda b, t: (0, 0))],
        out_specs=[pl.BlockSpec((tt, W), lambda b, t: (b * nt + t, 0)),
                   pl.BlockSpec((1, POOL_HIST, W), lambda b, t: (b, 0, 0))],
        out_shape=[jax.ShapeDtypeStruct((B * L, W), BF16),
                   jax.ShapeDtypeStruct((B, POOL_HIST, W), F32)],
        scratch_shapes=[pltpu.VMEM((tt + POOL_HIST + 1, W), F32)],
        compiler_params=_params("parallel", "arbitrary"),
        name="pool_mixer",
    )(proj, proj, hist16, w_bf16, scale.reshape(1, W))


HG_HEADS_PER_STEP = 16


def _hgrn_kernel(q_ref, f_ref, v_ref, g_ref, lb_ref, ng_ref, s0_ref, o_ref, s_ref, st_ref, oacc_ref, *, C):
    ci = pl.program_id(2)
    nc = pl.num_programs(2)
    hb = HG_HEADS_PER_STEP
    nsub = C // HG_BLOCK
    ref_row = HG_BLOCK // 2 - 1

    @pl.when(ci == 0)
    def _():
        for h in range(hb):
            st_ref[h] = s0_ref[0, h].T

    lb = lb_ref[...]
    f = lb + (1.0 - lb) * jax.nn.sigmoid(f_ref[...])
    logf = jnp.log(f)
    kk = 1.0 - f
    tri = (lax.broadcasted_iota(jnp.int32, (C, C), 1)
           <= lax.broadcasted_iota(jnp.int32, (C, C), 0)).astype(F32)
    bcum = jnp.dot(tri, logf, precision=lax.Precision.HIGHEST, preferred_element_type=F32)
    blast = bcum[C - 1:C, :]
    q = q_ref[...]
    v = v_ref[...]
    qd = (q * jnp.exp(bcum)).astype(BF16)
    kdec = (kk * jnp.exp(blast - bcum)).astype(BF16)
    eb = jnp.exp(blast)
    vb = v.astype(BF16)
    gate = jax.nn.silu(g_ref[...])
    ng = ng_ref[...]
    for h in range(hb):
        sl = slice(h * HG_DK, (h + 1) * HG_DK)
        st = st_ref[h]
        oacc_ref[...] = lax.dot_general(qd[:, sl], st.astype(BF16), (((1,), (1,)), ((), ())),
                                        preferred_element_type=F32)
        for j in range(nsub):
            r0 = j * HG_BLOCK
            bj = bcum[r0:, sl]
            bref = bcum[r0 + ref_row:r0 + ref_row + 1, sl]
            qj = (q[r0:, sl] * jnp.exp(bj - bref)).astype(BF16)
            kj = (kk[r0:r0 + HG_BLOCK, sl] * jnp.exp(bref - bj[:HG_BLOCK])).astype(BF16)
            att = lax.dot_general(qj, kj, (((1,), (1,)), ((), ())), preferred_element_type=F32)
            rr = lax.broadcasted_iota(jnp.int32, att.shape, 0)
            cc = lax.broadcasted_iota(jnp.int32, att.shape, 1)
            att = jnp.where(cc <= rr, att, 0.0)
            oacc_ref[r0:, :] += _dot(att.astype(BF16), vb[r0:r0 + HG_BLOCK, sl])
        o = oacc_ref[...]
        o = o * lax.rsqrt(jnp.mean(o * o, axis=-1, keepdims=True) + EPS)
        o = o * ng[:, sl]
        o_ref[:, sl] = (o * gate[:, sl]).astype(o_ref.dtype)
        upd = lax.dot_general(vb[:, sl], kdec[:, sl], (((0,), (0,)), ((), ())),
                              preferred_element_type=F32)
        st_ref[h] = eb[:, sl] * st + upd

    @pl.when(ci == nc - 1)
    def _():
        for h in range(hb):
            s_ref[0, h] = st_ref[h].T


def hgrn2_mixer(proj, lb, norm_g, s0, *, B, L):
    W = W_BRANCH
    C = min(L, 128)
    nc = L // C
    hb = HG_HEADS_PER_STEP
    nh = HG_HEADS // hb
    wb = hb * HG_DK
    per = W // wb

    def col(k):
        return pl.BlockSpec((C, wb), lambda b, h, c: (b * nc + c, k * per + h))

    vec = pl.BlockSpec((1, wb), lambda b, h, c: (0, h))
    st = pl.BlockSpec((1, hb, HG_DK, HG_DV), lambda b, h, c: (b, h, 0, 0))
    kern = functools.partial(_hgrn_kernel, C=C)
    return pl.pallas_call(
        kern,
        grid=(B, nh, nc),
        in_specs=[col(1), col(2), col(3), col(4), vec, vec, st],
        out_specs=[pl.BlockSpec((C, wb), lambda b, h, c: (b * nc + c, h)), st],
        out_shape=[jax.ShapeDtypeStruct((B * L, W), BF16),
                   jax.ShapeDtypeStruct((B, HG_HEADS, HG_DK, HG_DV), F32)],
        scratch_shapes=[pltpu.VMEM((hb, HG_DV, HG_DK), F32),
                        pltpu.VMEM((C, HG_DV), F32)],
        compiler_params=_params("parallel", "parallel", "arbitrary"),
        name="hgrn2_mixer",
    )(proj, proj, proj, proj, lb.reshape(1, W), norm_g.reshape(1, W), s0)


S5_CHUNKS_PER_STEP = 4


def _s5_scan_tile(xr, xi, cr, cim, pw_ref, cc):
    for idx, k in enumerate((1, 2, 4)):
        sr = pltpu.roll(xr, k, 0)
        si = pltpu.roll(xi, k, 0)
        mr = pw_ref[cc, 2 * idx]
        mi = pw_ref[cc, 2 * idx + 1]
        xr, xi = xr + (mr * sr - mi * si), xi + (mr * si + mi * sr)
    pr = pw_ref[cc, 6]
    pim = pw_ref[cc, 7]
    return xr + (pr * cr - pim * cim), xi + (pr * cim + pim * cr)


def _s5_kernel(u_ref, bw_ref, cw_ref, pw_ref, x0_ref, d_ref, y_ref, xo_ref,
               xr0_s, xi0_s, xr1_s, xi1_s, car_s, *, nseq, seg, nt):
    ti = pl.program_id(2)
    bufs = ((xr0_s, xi0_s), (xr1_s, xi1_s))

    def bcast(row):
        return jnp.broadcast_to(row, (SUBLANES, S5_ST))

    if nt > 1:
        @pl.when(ti == 0)
        def _():
            for cc in range(S5_CHUNKS_PER_STEP):
                car_s[cc, 0] = bcast(x0_ref[0, cc, 0:1, :])
                car_s[cc, 1] = bcast(x0_ref[0, cc, 1:2, :])

    for cc in range(S5_CHUNKS_PER_STEP):
        xr_s, xi_s = bufs[cc % 2]
        lanes = slice(cc * S5_CH, (cc + 1) * S5_CH)
        u = u_ref[:, lanes]
        bu = _dot(u.astype(BF16), bw_ref[cc])
        xr_s[...] = bu[:, :S5_ST]
        xi_s[...] = bu[:, S5_ST:]
        for q in range(nseq):
            if nt > 1:
                cr, cim = car_s[cc, 0], car_s[cc, 1]
            else:
                cr, cim = bcast(x0_ref[q, cc, 0:1, :]), bcast(x0_ref[q, cc, 1:2, :])
            for r in range(seg // SUBLANES):
                rows = slice(q * seg + r * SUBLANES, q * seg + (r + 1) * SUBLANES)
                xr, xi = _s5_scan_tile(xr_s[rows, :], xi_s[rows, :], cr, cim, pw_ref, cc)
                xr_s[rows, :] = xr
                xi_s[rows, :] = xi
                cr, cim = bcast(xr[SUBLANES - 1:SUBLANES]), bcast(xi[SUBLANES - 1:SUBLANES])
            if nt > 1:
                car_s[cc, 0] = cr
                car_s[cc, 1] = cim
            else:
                xo_ref[q, cc] = jnp.concatenate([cr[0:1], cim[0:1]], axis=0)
        y = _dot(xr_s[...].astype(BF16), cw_ref[cc, 0]) + _dot(xi_s[...].astype(BF16), cw_ref[cc, 1])
        y = y + d_ref[:, lanes] * u
        y_ref[:, lanes] = jax.nn.gelu(y).astype(y_ref.dtype)

    if nt > 1:
        @pl.when(ti == nt - 1)
        def _():
            for cc in range(S5_CHUNKS_PER_STEP):
                xo_ref[0, cc] = jnp.concatenate([car_s[cc, 0, 0:1, :], car_s[cc, 1, 0:1, :]], axis=0)


def _s5_discretize(a_re, a_im, log_step, b_re, b_im, c_re, c_im):
    dt = jnp.exp(log_step)[:, None]
    mag = jnp.exp(dt * a_re)
    ab_re = mag * jnp.cos(dt * a_im)
    ab_im = mag * jnp.sin(dt * a_im)
    den = a_re * a_re + a_im * a_im
    coef_re = ((ab_re - 1.0) * a_re + ab_im * a_im) / den
    coef_im = (ab_im * a_re - (ab_re - 1.0) * a_im) / den
    bb_re = coef_re[..., None] * b_re - coef_im[..., None] * b_im
    bb_im = coef_re[..., None] * b_im + coef_im[..., None] * b_re
    eye = jnp.eye(S5_CHUNK_GROUPS, dtype=F32)

    def blockdiag_b(bb):
        t = bb.reshape(S5_CHUNKS, S5_CHUNK_GROUPS, S5_STATE, S5_GROUP).transpose(0, 1, 3, 2)
        return jnp.einsum('cgxp,gh->cgxhp', t, eye).reshape(S5_CHUNKS, S5_CH, S5_ST)

    def blockdiag_c(cc):
        t = cc.reshape(S5_CHUNKS, S5_CHUNK_GROUPS, S5_GROUP, S5_STATE)
        return jnp.einsum('cgxp,gh->cgphx', t, eye).reshape(S5_CHUNKS, S5_ST, S5_CH)

    bw = jnp.concatenate([blockdiag_b(bb_re), blockdiag_b(bb_im)], axis=2).astype(BF16)
    cw = jnp.stack([blockdiag_c(c_re), -blockdiag_c(c_im)], axis=1).astype(BF16)

    pows = [(ab_re, ab_im)]
    for _ in range(SUBLANES - 1):
        pr, pi = pows[-1]
        pows.append((pr * ab_re - pi * ab_im, pr * ab_im + pi * ab_re))

    def plane(vals):
        return jnp.stack([v.reshape(S5_CHUNKS, S5_ST) for v in vals], axis=1)

    zero = jnp.zeros_like(ab_re)
    planes = []
    for k in (1, 2, 4):
        planes.append(plane([pows[k - 1][0] if t >= k else zero for t in range(SUBLANES)]))
        planes.append(plane([pows[k - 1][1] if t >= k else zero for t in range(SUBLANES)]))
    planes.append(plane([pows[t][0] for t in range(SUBLANES)]))
    planes.append(plane([pows[t][1] for t in range(SUBLANES)]))
    pw = jnp.stack(planes, axis=1)
    return bw, cw, pw


def s5_mixer(proj, x0_re, x0_im, disc, d_skip, *, B, L):
    W = W_BRANCH
    bw, cw, pw = disc
    cps = S5_CHUNKS_PER_STEP
    n_planes = pw.shape[1]
    if L >= 512:
        nseq, seg, nt, bsteps = 1, 512, L // 512, B
    else:
        nseq, seg, nt, bsteps = B, L, 1, 1
    kern = functools.partial(_s5_kernel, nseq=nseq, seg=seg, nt=nt)
    x_scratch = pltpu.VMEM((nseq * seg, S5_ST), F32)
    rows = nseq * seg
    col0 = 5 * W // (cps * S5_CH)
    x0 = jnp.stack([x0_re.reshape(B, S5_CHUNKS, S5_ST), x0_im.reshape(B, S5_CHUNKS, S5_ST)], axis=2)
    state = pl.BlockSpec((nseq, cps, 2, S5_ST), lambda b, c, t: (b, c, 0, 0))
    y, xo = pl.pallas_call(
        kern,
        grid=(bsteps, S5_CHUNKS // cps, nt),
        in_specs=[pl.BlockSpec((rows, cps * S5_CH), lambda b, c, t: (b * nt + t, col0 + c)),
                  pl.BlockSpec((cps, S5_CH, 2 * S5_ST), lambda b, c, t: (c, 0, 0)),
                  pl.BlockSpec((cps, 2, S5_ST, S5_CH), lambda b, c, t: (c, 0, 0, 0)),
                  pl.BlockSpec((cps, n_planes, SUBLANES, S5_ST), lambda b, c, t: (c, 0, 0, 0)),
                  state,
                  pl.BlockSpec((1, cps * S5_CH), lambda b, c, t: (0, c))],
        out_specs=[pl.BlockSpec((rows, cps * S5_CH), lambda b, c, t: (b * nt + t, c)), state],
        out_shape=[jax.ShapeDtypeStruct((B * L, W), BF16),
                   jax.ShapeDtypeStruct((B, S5_CHUNKS, 2, S5_ST), F32)],
        scratch_shapes=[x_scratch] * 4 + [pltpu.VMEM((cps, 2, SUBLANES, S5_ST), F32)],
        compiler_params=_params("parallel", "parallel", "arbitrary"),
        name="s5_scan",
    )(proj, bw, cw, pw, x0, d_skip.reshape(1, W))
    s_re = xo[:, :, 0, :].reshape(B, S5_GROUPS, S5_STATE)
    s_im = xo[:, :, 1, :].reshape(B, S5_GROUPS, S5_STATE)
    return y, s_re, s_im


def _glu_kernel(a_ref, b_ref, y_ref, bias_ref, o_ref):
    a = a_ref[...]
    for cols in _column_pieces(o_ref.shape[1]):
        z = _dot(a, b_ref[:, cols]) + bias_ref[:, cols]
        o_ref[:, cols] = (y_ref[:, cols].astype(F32) * jax.nn.sigmoid(z)).astype(o_ref.dtype)


def s5_glu(y, w_bf16, l, bias, *, tm, tn):
    M, K = y.shape
    N = w_bf16.shape[-1]
    return pl.pallas_call(
        _glu_kernel,
        grid=(M // tm, N // tn),
        in_specs=[pl.BlockSpec((tm, K), lambda i, j: (i, 0)),
                  pl.BlockSpec((None, K, tn), lambda i, j: (l, 0, j)),
                  pl.BlockSpec((tm, tn), lambda i, j: (i, j)),
                  pl.BlockSpec((1, tn), lambda i, j: (0, j))],
        out_specs=pl.BlockSpec((tm, tn), lambda i, j: (i, j)),
        out_shape=jax.ShapeDtypeStruct((M, N), BF16),
        compiler_params=_params("parallel", "arbitrary"),
        name="s5_glu",
    )(y, w_bf16, y, bias.reshape(1, N))


def _merge_kernel(oa_ref, ob_ref, oc_ref, w_ref, ga_ref, gb_ref, gc_ref, o_ref):
    acc = jax.nn.sigmoid(ga_ref[...].astype(F32)) * _dot(oa_ref[...], w_ref[0])
    acc = acc + jax.nn.sigmoid(gb_ref[...].astype(F32)) * _dot(ob_ref[...], w_ref[1])
    acc = acc + jax.nn.sigmoid(gc_ref[...].astype(F32)) * _dot(oc_ref[...], w_ref[2])
    o_ref[...] = acc.astype(o_ref.dtype)


def branch_merge(o_a, o_b, o_c, w_br_bf16, l, gates, *, tm, tn):
    M, W = o_a.shape
    D = D_MODEL
    nj = D // tn
    act = pl.BlockSpec((tm, W), lambda i, j: (i, 0))

    def gate(n):
        return pl.BlockSpec((tm, tn), lambda i, j: (i, n * nj + j))

    return pl.pallas_call(
        _merge_kernel,
        grid=(M // tm, nj),
        in_specs=[act, act, act,
                  pl.BlockSpec((None, N_BRANCH, W, tn), lambda i, j: (l, 0, 0, j)),
                  gate(0), gate(1), gate(2)],
        out_specs=pl.BlockSpec((tm, tn), lambda i, j: (i, j)),
        out_shape=jax.ShapeDtypeStruct((M, D), BF16),
        compiler_params=_params("parallel", "arbitrary"),
        name="branch_merge",
    )(o_a, o_b, o_c, w_br_bf16, gates, gates, gates)


def _conv_gate(ua0, ua1, ua2, uv0, uv1, uv2, cwa_ref, cwv_ref, cba_ref, cbv_ref):
    shape = ua0.shape
    tn = shape[-1]

    def tiles(x):
        return x.reshape(-1, SUBLANES, tn)

    def tap(ref, k):
        return ref[k * SUBLANES:(k + 1) * SUBLANES, :][None]

    a = cba_ref[...][None] + (tap(cwa_ref, 0) * tiles(ua0) + tap(cwa_ref, 1) * tiles(ua1) + tap(cwa_ref, 2) * tiles(ua2))
    v = cbv_ref[...][None] + (tap(cwv_ref, 0) * tiles(uv0) + tap(cwv_ref, 1) * tiles(uv1) + tap(cwv_ref, 2) * tiles(uv2))
    return (jax.nn.gelu(a) * v).reshape(shape)


def _up_conv_long_kernel(x_ref, xp_ref, wal_ref, wah_ref, wvl_ref, wvh_ref, cwa_ref, cwv_ref, cba_ref, cbv_ref,
                         hist_ref, g_ref, tail_ref, xe_s, *, tm, tiles_per_seq):
    i = pl.program_id(0)
    j = pl.program_id(1)
    pad = BF16_ROWS
    half = wal_ref.shape[1]

    @pl.when(j == 0)
    def _():
        xe_s[0:pad, :] = xp_ref[...]
        xe_s[pad:, :] = x_ref[...]

    xe = xe_s[...]
    seq_start = (i % tiles_per_seq) == 0
    use_hist = (lax.broadcasted_iota(jnp.int32, (pad, half), 0) >= pad - 2) & seq_start

    def with_history(u, hist_rows):
        return jnp.concatenate([jnp.where(use_hist, hist_rows, u[:pad]), u[pad:]], axis=0)

    def taps(u):
        return pltpu.roll(u, 2, 0)[pad:], pltpu.roll(u, 1, 0)[pad:], u[pad:]

    def half_tile(c):
        wa_ref, wv_ref = ((wal_ref, wvl_ref), (wah_ref, wvh_ref))[c]
        cols = slice(c * half, (c + 1) * half)
        ua = with_history(_dot(xe, wa_ref[...]), hist_ref[0, 0, :, cols])
        uv = with_history(_dot(xe, wv_ref[...]), hist_ref[1, 0, :, cols])
        g = _conv_gate(*taps(ua), *taps(uv), cwa_ref[:, cols], cwv_ref[:, cols], cba_ref[:, cols], cbv_ref[:, cols])
        g_ref[:, cols] = g.astype(g_ref.dtype)
        tail_ref[0, 0, :, cols] = ua[pad + tm - 2:]
        tail_ref[1, 0, :, cols] = uv[pad + tm - 2:]

    last = pl.num_programs(1) - 1

    @pl.when(j < last)
    def _():
        half_tile(0)
        half_tile(1)

    @pl.when(j == last)
    def _():
        half_tile(0)


def _up_conv_short_kernel(x_ref, wal_ref, wah_ref, wvl_ref, wvh_ref, cwa_ref, cwv_ref, cba_ref, cbv_ref, hist_ref,
                          g_ref, tail_ref, ua_s, uv_s, *, nseq, L):
    x = x_ref[...]
    half = wal_ref.shape[1]
    tn = 2 * half
    pad = SUBLANES
    ua_s[:, pad:, :half] = _dot(x, wal_ref[...]).reshape(nseq, L, half)
    ua_s[:, pad:, half:] = _dot(x, wah_ref[...]).reshape(nseq, L, half)
    uv_s[:, pad:, :half] = _dot(x, wvl_ref[...]).reshape(nseq, L, half)
    uv_s[:, pad:, half:] = _dot(x, wvh_ref[...]).reshape(nseq, L, half)
    ua_s[:, pad - 2:pad, :] = hist_ref[0]
    uv_s[:, pad - 2:pad, :] = hist_ref[1]
    g = _conv_gate(ua_s[:, pad - 2:pad - 2 + L, :], ua_s[:, pad - 1:pad - 1 + L, :], ua_s[:, pad:pad + L, :],
                   uv_s[:, pad - 2:pad - 2 + L, :], uv_s[:, pad - 1:pad - 1 + L, :], uv_s[:, pad:pad + L, :],
                   cwa_ref, cwv_ref, cba_ref, cbv_ref)
    g_ref[...] = g.reshape(nseq * L, tn).astype(g_ref.dtype)
    tail_ref[0] = ua_s[:, pad + L - 2:pad + L, :]
    tail_ref[1] = uv_s[:, pad + L - 2:pad + L, :]


def conv_ffn_up(xn, w_up, l, conv_w, conv_b, hist, *, B, L):
    T, D = xn.shape
    tn = 512
    half = tn // 2
    nj = pl.cdiv(D_FF, tn)
    assert D_FF % tn == half
    voff = D_FF // half
    last = 2 * D_FF // half - 1
    cw8 = jnp.repeat(conv_w, SUBLANES, axis=0)
    cb8 = jnp.broadcast_to(conv_b[None, :], (SUBLANES, 2 * D_FF))
    cwa, cwv = cw8[:, :D_FF], cw8[:, D_FF:]
    cba, cbv = cb8[:, :D_FF], cb8[:, D_FF:]
    hist2 = jnp.stack([hist[:, :, :D_FF], hist[:, :, D_FF:]], axis=0)
    if L >= 1024:
        tm = 1024
        tps = L // tm
        nti = T // tm

        def wspec(k):
            return pl.BlockSpec((None, D, half), lambda i, j: (l, 0, jnp.minimum(2 * j + k, last)))

        cwspec = pl.BlockSpec((CONV_W * SUBLANES, tn), lambda i, j: (0, j))
        cbspec = pl.BlockSpec((SUBLANES, tn), lambda i, j: (0, j))
        kern = functools.partial(_up_conv_long_kernel, tm=tm, tiles_per_seq=tps)
        g, tails = pl.pallas_call(
            kern,
            grid=(nti, nj),
            in_specs=[pl.BlockSpec((tm, D), lambda i, j: (i, 0), pipeline_mode=pl.Buffered(1)),
                      pl.BlockSpec((BF16_ROWS, D),
                                   lambda i, j: (jnp.maximum(i * (tm // BF16_ROWS) - 1, 0), 0)),
                      wspec(0), wspec(1), wspec(voff), wspec(voff + 1),
                      cwspec, cwspec, cbspec, cbspec,
                      pl.BlockSpec((2, 1, BF16_ROWS, tn), lambda i, j: (0, i // tps, 0, j))],
            out_specs=[pl.BlockSpec((tm, tn), lambda i, j: (i, j)),
                       pl.BlockSpec((2, 1, 2, tn), lambda i, j: (0, i, 0, j))],
            out_shape=[jax.ShapeDtypeStruct((T, D_FF), BF16),
                       jax.ShapeDtypeStruct((2, nti, 2, D_FF), F32)],
            scratch_shapes=[pltpu.VMEM((tm + BF16_ROWS, D), BF16)],
            compiler_params=_params("parallel", "arbitrary"),
            name="conv_ffn_up_long",
        )(xn, xn, w_up, w_up, w_up, w_up, cwa, cwv, cba, cbv,
          jnp.pad(hist2, ((0, 0), (0, 0), (BF16_ROWS - 2, 0), (0, 0))))
        tails = tails[:, tps - 1::tps]
    else:
        def wspec(k):
            return pl.BlockSpec((None, D, half), lambda j: (l, 0, jnp.minimum(2 * j + k, last)))

        kern = functools.partial(_up_conv_short_kernel, nseq=B, L=L)
        g, tails = pl.pallas_call(
            kern,
            grid=(nj,),
            in_specs=[pl.BlockSpec((T, D), lambda j: (0, 0)),
                      wspec(0), wspec(1), wspec(voff), wspec(voff + 1),
                      pl.BlockSpec((CONV_W * SUBLANES, tn), lambda j: (0, j)),
                      pl.BlockSpec((CONV_W * SUBLANES, tn), lambda j: (0, j)),
                      pl.BlockSpec((SUBLANES, tn), lambda j: (0, j)),
                      pl.BlockSpec((SUBLANES, tn), lambda j: (0, j)),
                      pl.BlockSpec((2, B, 2, tn), lambda j: (0, 0, 0, j))],
            out_specs=[pl.BlockSpec((T, tn), lambda j: (0, j)),
                       pl.BlockSpec((2, B, 2, tn), lambda j: (0, 0, 0, j))],
            out_shape=[jax.ShapeDtypeStruct((T, D_FF), BF16),
                       jax.ShapeDtypeStruct((2, B, 2, D_FF), F32)],
            scratch_shapes=[pltpu.VMEM((B, L + SUBLANES, tn), F32),
                            pltpu.VMEM((B, L + SUBLANES, tn), F32)],
            compiler_params=_params("arbitrary"),
            name="conv_ffn_up_short",
        )(xn, w_up, w_up, w_up, w_up, cwa, cwv, cba, cbv, hist2)
    s_conv = jnp.concatenate([tails[0], tails[1]], axis=-1)
    return g, s_conv


def _ple_kernel(x_ref, wg_ref, p_ref, wp_ref, h_ref, o_ref):
    x = x_ref[...]
    pb = p_ref[...].astype(BF16)
    for cols in _column_pieces(o_ref.shape[1]):
        gate = jax.nn.sigmoid(_dot(x, wg_ref[:, cols]))
        emb = _dot(pb, wp_ref[:, cols])
        o_ref[:, cols] = h_ref[:, cols] + gate * emb


def ple_update(xn, w_pg, p, w_ple, l, h, *, tm, tn):
    M, D = xn.shape
    return pl.pallas_call(
        _ple_kernel,
        grid=(M // tm, D // tn),
        in_specs=[pl.BlockSpec((tm, D), lambda i, j: (i, 0)),
                  pl.BlockSpec((None, D, tn), lambda i, j: (l, 0, j)),
                  pl.BlockSpec((tm, PLE_DIM), lambda i, j: (i, 0)),
                  pl.BlockSpec((None, PLE_DIM, tn), lambda i, j: (l, 0, j)),
                  pl.BlockSpec((tm, tn), lambda i, j: (i, j))],
        out_specs=pl.BlockSpec((tm, tn), lambda i, j: (i, j)),
        out_shape=jax.ShapeDtypeStruct((M, D), F32),
        input_output_aliases={4: 0},
        compiler_params=_params("parallel", "arbitrary"),
        name="ple_update",
    )(xn, w_pg, p, w_ple, h)


def _layer_stack(x, p, st_pool, st_hg, st_s5r, st_s5i, st_conv, pos0, lb, wts):
    B, L, D = x.shape
    T = B * L
    W = W_BRANCH
    tm = min(T, 1024)
    h = x.reshape(T, D)
    n_pool, n_hg, n_s5r, n_s5i, n_conv = [], [], [], [], []
    w = wts
    for l in range(DEPTH):
        xn = rms_norm(h, w['norm_mix'][l], BF16)
        proj = matmul(xn, w['w_in'], l, F32, tm=tm, tn=1024, col_block0=0, n_cols=N_MIX, name="in_proj_mix")
        gates = matmul(xn, w['w_in'], l, BF16, tm=tm, tn=1024, col_block0=N_MIX // 1024,
                       n_cols=N_BRANCH * D, name="in_proj_gates")
        o_a, s_pool = pool_mixer(proj, st_pool[l], w['pool_w'], l, w['pool_scale'][l], B=B, L=L, pos0=pos0)
        o_b, s_hg = hgrn2_mixer(proj, lb[l], w['hg_norm'][l], st_hg[l], B=B, L=L)
        y_c, s_r, s_i = s5_mixer(proj, st_s5r[l], st_s5i[l], w['s5_disc'][l], w['s5_d'][l], B=B, L=L)
        o_c = s5_glu(y_c, w['s5_w_glu'], l, w['s5_b_glu'][l], tm=tm, tn=1024)
        merged = branch_merge(o_a, o_b, o_c, w['w_br'], l, gates, tm=tm, tn=512)
        h = matmul_residual(merged, w['w_out'], l, h, tm=tm, tn=1024, name="out_proj", in_place=l > 0)
        xn = rms_norm(h, w['norm_ffn'][l], BF16)
        g, s_conv = conv_ffn_up(xn, w['w_up'], l, w['conv_w'][l], w['conv_b'][l], st_conv[l], B=B, L=L)
        h = matmul_residual(g, w['w_down'], l, h, tm=min(T, 512), tn=512, name="ffn_down")
        xn = rms_norm(h, w['norm_ple'][l], BF16)
        h = ple_update(xn, w['w_ple_gate'], p[l].reshape(T, PLE_DIM), w['w_ple'], l, h, tm=tm, tn=512)
        n_pool.append(s_pool)
        n_hg.append(s_hg)
        n_s5r.append(s_r)
        n_s5i.append(s_i)
        n_conv.append(s_conv)
    y = rms_norm(h, w['norm_final'], F32).reshape(B, L, D)
    return (y, jnp.stack(n_pool), jnp.stack(n_hg), jnp.stack(n_s5r), jnp.stack(n_s5i), jnp.stack(n_conv))


def kernel(x_prompt, x_sample, state_pool, state_hgrn, state_s5_re, state_s5_im, state_ffn_conv,
           p_prompt, p_sample, norm_mix, w_in, pool_w, pool_scale, hg_lb_logits, hg_norm,
           s5_a_re, s5_a_im, s5_log_step, s5_b_re, s5_b_im, s5_c_re, s5_c_im, s5_d, s5_w_glu, s5_b_glu,
           w_br, w_out, norm_ffn, w_up, conv_w, conv_b, w_down, norm_ple, w_ple_gate, w_ple, norm_final):
    lb = forget_lower_bounds(hg_lb_logits)
    wts = {
        'norm_mix': norm_mix, 'w_in': cast_bf16(w_in), 'pool_w': cast_bf16(pool_w),
        'pool_scale': pool_scale, 'hg_norm': hg_norm,
        's5_disc': [_s5_discretize(s5_a_re[l], s5_a_im[l], s5_log_step[l], s5_b_re[l], s5_b_im[l],
                                   s5_c_re[l], s5_c_im[l]) for l in range(DEPTH)],
        's5_d': s5_d, 's5_w_glu': cast_bf16(s5_w_glu), 's5_b_glu': s5_b_glu,
        'w_br': cast_bf16(w_br), 'w_out': cast_bf16(w_out), 'norm_ffn': norm_ffn,
        'w_up': cast_bf16(w_up), 'conv_w': conv_w, 'conv_b': conv_b, 'w_down': cast_bf16(w_down),
        'norm_ple': norm_ple, 'w_ple_gate': cast_bf16(w_ple_gate), 'w_ple': cast_bf16(w_ple),
        'norm_final': norm_final,
    }
    bp = x_prompt.shape[0]
    z_pool = jnp.zeros((DEPTH, bp) + state_pool.shape[2:], F32)
    z_hg = jnp.zeros((DEPTH, bp) + state_hgrn.shape[2:], F32)
    z_s5 = jnp.zeros((DEPTH, bp) + state_s5_re.shape[2:], F32)
    z_conv = jnp.zeros((DEPTH, bp) + state_ffn_conv.shape[2:], F32)
    y_p, pool_p, hg_p, s5r_p, s5i_p, conv_p = _layer_stack(
        x_prompt, p_prompt, z_pool, z_hg, z_s5, z_s5, z_conv, 0, lb, wts)
    y_s, pool_s, hg_s, s5r_s, s5i_s, conv_s = _layer_stack(
        x_sample, p_sample, state_pool, state_hgrn, state_s5_re, state_s5_im, state_ffn_conv,
        PAST_LEN, lb, wts)
    return (y_p, y_s, pool_p, hg_p, s5r_p, s5i_p, conv_p, pool_s, hg_s, s5r_s, s5i_s, conv_s)
```

```python
import functools

import jax
import jax.numpy as jnp
from jax import lax
from jax.experimental import pallas as pl
from jax.experimental.pallas import tpu as pltpu

D_MODEL = 4096
DEPTH = 4
PAST_LEN = 1024
W_BRANCH = D_MODEL // 2
N_BRANCH = 3
POOL_WINDOWS = (2, 4, 8, 16)
POOL_GROUP = W_BRANCH // len(POOL_WINDOWS)
POOL_HIST = max(POOL_WINDOWS) - 1
HG_DK = 128
HG_HEADS = W_BRANCH // HG_DK
HG_DV = W_BRANCH // HG_HEADS
HG_BLOCK = 16
S5_GROUP = 16
S5_GROUPS = W_BRANCH // S5_GROUP
S5_STATE = 64
D_FF = 11008
CONV_W = 3
PLE_DIM = 256
N_MIX = 6 * W_BRANCH
N_IN = N_MIX + N_BRANCH * D_MODEL
EPS = 1e-6

F32 = jnp.float32
BF16 = jnp.bfloat16

VMEM_LIMIT_BYTES = 56 * 1024 * 1024
SUBLANES = 8
BF16_ROWS = 16

S5_CHUNK_GROUPS = 16
S5_CHUNKS = S5_GROUPS // S5_CHUNK_GROUPS
S5_CH = S5_CHUNK_GROUPS * S5_GROUP
S5_ST = S5_CHUNK_GROUPS * S5_STATE


def _params(*sem, flags=None):
    return pltpu.CompilerParams(dimension_semantics=sem, vmem_limit_bytes=VMEM_LIMIT_BYTES, flags=flags)


def _dot(a, b):
    return jnp.dot(a, b, preferred_element_type=F32)


MXU_COLUMNS = 256


def _column_pieces(n, width=MXU_COLUMNS):
    return [slice(c, c + width) for c in range(0, n, width)]


def _norm_kernel(x_ref, g_ref, o_ref):
    x = x_ref[...]
    ms = jnp.mean(x * x, axis=-1, keepdims=True)
    o_ref[...] = ((x * lax.rsqrt(ms + EPS)) * g_ref[...]).astype(o_ref.dtype)


def rms_norm(x, g, out_dtype):
    T, D = x.shape
    tr = min(T, 512)
    return pl.pallas_call(
        _norm_kernel,
        grid=(T // tr,),
        in_specs=[pl.BlockSpec((tr, D), lambda i: (i, 0)),
                  pl.BlockSpec((1, D), lambda i: (0, 0))],
        out_specs=pl.BlockSpec((tr, D), lambda i: (i, 0)),
        out_shape=jax.ShapeDtypeStruct((T, D), out_dtype),
        compiler_params=_params("parallel"),
        name="rms_norm",
    )(x, g.reshape(1, D))


def _cast_kernel(x_ref, o_ref):
    o_ref[...] = x_ref[...].astype(o_ref.dtype)


def cast_bf16(w):
    shape = w.shape
    N = shape[-1]
    R = w.size // N
    tr = 1024
    tc = min(N, 2048)
    out = pl.pallas_call(
        _cast_kernel,
        grid=(R // tr, pl.cdiv(N, tc)),
        in_specs=[pl.BlockSpec((tr, tc), lambda i, j: (i, j))],
        out_specs=pl.BlockSpec((tr, tc), lambda i, j: (i, j)),
        out_shape=jax.ShapeDtypeStruct((R, N), BF16),
        compiler_params=_params("parallel", "parallel"),
        name="cast_bf16",
    )(w.reshape(R, N))
    return out.reshape(shape)


def _mm_kernel(a_ref, b_ref, o_ref):
    o_ref[...] = _dot(a_ref[...], b_ref[...]).astype(o_ref.dtype)


def matmul(a, b, l, out_dtype, *, tm, tn, col_block0, n_cols, name):
    M, K = a.shape
    return pl.pallas_call(
        _mm_kernel,
        grid=(M // tm, n_cols // tn),
        in_specs=[pl.BlockSpec((tm, K), lambda i, j: (i, 0)),
                  pl.BlockSpec((None, K, tn), lambda i, j: (l, 0, j + col_block0))],
        out_specs=pl.BlockSpec((tm, tn), lambda i, j: (i, j)),
        out_shape=jax.ShapeDtypeStruct((M, n_cols), out_dtype),
        compiler_params=_params("parallel", "arbitrary"),
        name=name,
    )(a, b)


def _mm_res_kernel(a_ref, b_ref, h_ref, o_ref):
    o_ref[...] = h_ref[...] + _dot(a_ref[...], b_ref[...])


def matmul_residual(a, b, l, h, *, tm, tn, name, in_place=True):
    M, K = a.shape
    N = b.shape[-1]
    return pl.pallas_call(
        _mm_res_kernel,
        grid=(M // tm, N // tn),
        in_specs=[pl.BlockSpec((tm, K), lambda i, j: (i, 0)),
                  pl.BlockSpec((None, K, tn), lambda i, j: (l, 0, j)),
                  pl.BlockSpec((tm, tn), lambda i, j: (i, j))],
        out_specs=pl.BlockSpec((tm, tn), lambda i, j: (i, j)),
        out_shape=jax.ShapeDtypeStruct((M, N), F32),
        input_output_aliases={2: 0} if in_place else {},
        compiler_params=_params("parallel", "arbitrary"),
        name=name,
    )(a, b, h)


def _lb_kernel(x_ref, o_ref):
    x = x_ref[...]
    m = jnp.max(x, axis=0, keepdims=True)
    e = jnp.exp(x - m)
    p = e / jnp.sum(e, axis=0, keepdims=True)
    rows = [p[0:1]]
    for l in range(1, DEPTH):
        rows.append(rows[-1] + p[l:l + 1])
    o_ref[...] = jnp.concatenate([r - rows[0] for r in rows], axis=0)


def forget_lower_bounds(logits):
    return pl.pallas_call(
        _lb_kernel,
        out_shape=jax.ShapeDtypeStruct(logits.shape, F32),
        name="hgrn_lower_bounds",
    )(logits)


def _pool_kernel(u_ref, prev_ref, hist_ref, w_ref, scale_ref, o_ref, st_ref, ext_ref, *, tt, pos0):
    ti = pl.program_id(1)
    nt = pl.num_programs(1)
    hist_rows = POOL_HIST + 1

    @pl.when(ti == 0)
    def _():
        ext_ref[0:hist_rows, :] = hist_ref[0]

    @pl.when(ti > 0)
    def _():
        ext_ref[0:hist_rows, :] = prev_ref[...]

    cur = u_ref[...]
    ext_ref[hist_rows:, :] = cur
    pos = pos0 + ti * tt + lax.broadcasted_iota(jnp.int32, (tt, 1), 0)
    for gi, w in enumerate(POOL_WINDOWS):
        cols = slice(gi * POOL_GROUP, (gi + 1) * POOL_GROUP)
        win = ext_ref[pl.ds(hist_rows, tt), cols]
        for k in range(1, w):
            win = win + ext_ref[pl.ds(hist_rows - k, tt), cols]
        cnt = jnp.minimum(pos + 1, w).astype(F32)
        pooled = win / cnt - cur[:, cols]
        y = _dot(pooled.astype(BF16), w_ref[gi]) * scale_ref[:, cols]
        o_ref[:, cols] = y.astype(o_ref.dtype)

    @pl.when(ti == nt - 1)
    def _():
        st_ref[0] = ext_ref[pl.ds(tt + 1, POOL_HIST), :]


def pool_mixer(proj, hist, w_bf16, l, scale, *, B, L, pos0):
    W = W_BRANCH
    tt = min(L, 256)
    nt = L // tt
    hist16 = jnp.concatenate([jnp.zeros((B, 1, W), F32), hist], axis=1)
    blk16 = tt // (POOL_HIST + 1)
    kern = functools.partial(_pool_kernel, tt=tt, pos0=pos0)
    return pl.pallas_call(
        kern,
        grid=(B, nt),
        in_specs=[pl.BlockSpec((tt, W), lambda b, t: (b * nt + t, 0)),
                  pl.BlockSpec((POOL_HIST + 1, W),
                               lambda b, t: (jnp.maximum((b * nt + t) * blk16 - 1, 0), 0)),
                  pl.BlockSpec((1, POOL_HIST + 1, W), lambda b, t: (b, 0, 0)),
                  pl.BlockSpec((None, len(POOL_WINDOWS), POOL_GROUP, POOL_GROUP), lambda b, t: (l, 0, 0, 0)),
                  pl.BlockSpec((1, W), lambda b, t: (0, 0))],
        out_specs=[pl.BlockSpec((tt, W), lambda b, t: (b * nt + t, 0)),
                   pl.BlockSpec((1, POOL_HIST, W), lambda b, t: (b, 0, 0))],
        out_shape=[jax.ShapeDtypeStruct((B * L, W), BF16),
                   jax.ShapeDtypeStruct((B, POOL_HIST, W), F32)],
        scratch_shapes=[pltpu.VMEM((tt + POOL_HIST + 1, W), F32)],
        compiler_params=_params("parallel", "arbitrary"),
        name="pool_mixer",
    )(proj, proj, hist16, w_bf16, scale.reshape(1, W))


HG_HEADS_PER_STEP = 16


def _hgrn_kernel(q_ref, f_ref, v_ref, g_ref, lb_ref, ng_ref, s0_ref, o_ref, s_ref, st_ref, oacc_ref, *, C):
    ci = pl.program_id(2)
    nc = pl.num_programs(2)
    hb = HG_HEADS_PER_STEP
    nsub = C // HG_BLOCK
    ref_row = HG_BLOCK // 2 - 1

    @pl.when(ci == 0)
    def _():
        for h in range(hb):
            st_ref[h] = s0_ref[0, h].T

    lb = lb_ref[...]
    f = lb + (1.0 - lb) * jax.nn.sigmoid(f_ref[...])
    logf = jnp.log(f)
    kk = 1.0 - f
    tri = (lax.broadcasted_iota(jnp.int32, (C, C), 1)
           <= lax.broadcasted_iota(jnp.int32, (C, C), 0)).astype(BF16)
    w = logf.shape[1]
    hi = logf.astype(BF16)
    rest = logf - hi.astype(F32)
    mid = rest.astype(BF16)
    lo = (rest - mid.astype(F32)).astype(BF16)
    parts = _dot(tri, jnp.concatenate([hi, mid, lo], axis=1))
    bcum = parts[:, :w] + (parts[:, w:2 * w] + parts[:, 2 * w:])
    blast = bcum[C - 1:C, :]
    q = q_ref[...]
    v = v_ref[...]
    qd = (q * jnp.exp(bcum)).astype(BF16)
    kdec = (kk * jnp.exp(blast - bcum)).astype(BF16)
    eb = jnp.exp(blast)
    vb = v.astype(BF16)
    gate = jax.nn.silu(g_ref[...])
    ng = ng_ref[...]
    for h in range(hb):
        sl = slice(h * HG_DK, (h + 1) * HG_DK)
        st = st_ref[h]
        oacc_ref[...] = lax.dot_general(qd[:, sl], st.astype(BF16), (((1,), (1,)), ((), ())),
                                        preferred_element_type=F32)
        for j in range(nsub):
            r0 = j * HG_BLOCK
            bj = bcum[r0:, sl]
            bref = bcum[r0 + ref_row:r0 + ref_row + 1, sl]
            qj = (q[r0:, sl] * jnp.exp(bj - bref)).astype(BF16)
            kj = (kk[r0:r0 + HG_BLOCK, sl] * jnp.exp(bref - bj[:HG_BLOCK])).astype(BF16)
            att = lax.dot_general(qj, kj, (((1,), (1,)), ((), ())), preferred_element_type=F32)
            rr = lax.broadcasted_iota(jnp.int32, att.shape, 0)
            cc = lax.broadcasted_iota(jnp.int32, att.shape, 1)
            att = jnp.where(cc <= rr, att, 0.0)
            oacc_ref[r0:, :] += _dot(att.astype(BF16), vb[r0:r0 + HG_BLOCK, sl])
        o = oacc_ref[...]
        o = o * lax.rsqrt(jnp.mean(o * o, axis=-1, keepdims=True) + EPS)
        o = o * ng[:, sl]
        o_ref[:, sl] = (o * gate[:, sl]).astype(o_ref.dtype)
        upd = lax.dot_general(vb[:, sl], kdec[:, sl], (((0,), (0,)), ((), ())),
                              preferred_element_type=F32)
        st_ref[h] = eb[:, sl] * st + upd

    @pl.when(ci == nc - 1)
    def _():
        for h in range(hb):
            s_ref[0, h] = st_ref[h].T


def hgrn2_mixer(proj, lb, norm_g, s0, *, B, L):
    W = W_BRANCH
    C = min(L, 128)
    nc = L // C
    hb = HG_HEADS_PER_STEP
    nh = HG_HEADS // hb
    wb = hb * HG_DK
    per = W // wb

    def col(k):
        return pl.BlockSpec((C, wb), lambda b, h, c: (b * nc + c, k * per + h))

    vec = pl.BlockSpec((1, wb), lambda b, h, c: (0, h))
    st = pl.BlockSpec((1, hb, HG_DK, HG_DV), lambda b, h, c: (b, h, 0, 0))
    kern = functools.partial(_hgrn_kernel, C=C)
    return pl.pallas_call(
        kern,
        grid=(B, nh, nc),
        in_specs=[col(1), col(2), col(3), col(4), vec, vec, st],
        out_specs=[pl.BlockSpec((C, wb), lambda b, h, c: (b * nc + c, h)), st],
        out_shape=[jax.ShapeDtypeStruct((B * L, W), BF16),
                   jax.ShapeDtypeStruct((B, HG_HEADS, HG_DK, HG_DV), F32)],
        scratch_shapes=[pltpu.VMEM((hb, HG_DV, HG_DK), F32),
                        pltpu.VMEM((C, HG_DV), F32)],
        compiler_params=_params("parallel", "parallel", "arbitrary"),
        name="hgrn2_mixer",
    )(proj, proj, proj, proj, lb.reshape(1, W), norm_g.reshape(1, W), s0)


S5_CHUNKS_PER_STEP = 4


def _s5_scan_tile(xr, xi, cr, cim, pw_ref, cc):
    for idx, k in enumerate((1, 2, 4)):
        sr = pltpu.roll(xr, k, 0)
        si = pltpu.roll(xi, k, 0)
        mr = pw_ref[cc, 2 * idx]
        mi = pw_ref[cc, 2 * idx + 1]
        xr, xi = xr + (mr * sr - mi * si), xi + (mr * si + mi * sr)
    pr = pw_ref[cc, 6]
    pim = pw_ref[cc, 7]
    return xr + (pr * cr - pim * cim), xi + (pr * cim + pim * cr)


def _s5_kernel(u_ref, bw_ref, cw_ref, pw_ref, x0_ref, d_ref, y_ref, xo_ref,
               xr0_s, xi0_s, xr1_s, xi1_s, car_s, *, nseq, seg, nt):
    ti = pl.program_id(2)
    bufs = ((xr0_s, xi0_s), (xr1_s, xi1_s))

    def bcast(row):
        return jnp.broadcast_to(row, (SUBLANES, S5_ST))

    if nt > 1:
        @pl.when(ti == 0)
        def _():
            for cc in range(S5_CHUNKS_PER_STEP):
                car_s[cc, 0] = bcast(x0_ref[0, cc, 0:1, :])
                car_s[cc, 1] = bcast(x0_ref[0, cc, 1:2, :])

    for cc in range(S5_CHUNKS_PER_STEP):
        xr_s, xi_s = bufs[cc % 2]
        lanes = slice(cc * S5_CH, (cc + 1) * S5_CH)
        u = u_ref[:, lanes]
        bu = _dot(u.astype(BF16), bw_ref[cc])
        xr_s[...] = bu[:, :S5_ST]
        xi_s[...] = bu[:, S5_ST:]
        for q in range(nseq):
            if nt > 1:
                cr, cim = car_s[cc, 0], car_s[cc, 1]
            else:
                cr, cim = bcast(x0_ref[q, cc, 0:1, :]), bcast(x0_ref[q, cc, 1:2, :])
            for r in range(seg // SUBLANES):
                rows = slice(q * seg + r * SUBLANES, q * seg + (r + 1) * SUBLANES)
                xr, xi = _s5_scan_tile(xr_s[rows, :], xi_s[rows, :], cr, cim, pw_ref, cc)
                xr_s[rows, :] = xr
                xi_s[rows, :] = xi
                cr, cim = bcast(xr[SUBLANES - 1:SUBLANES]), bcast(xi[SUBLANES - 1:SUBLANES])
            if nt > 1:
                car_s[cc, 0] = cr
                car_s[cc, 1] = cim
            else:
                xo_ref[q, cc] = jnp.concatenate([cr[0:1], cim[0:1]], axis=0)
        y = _dot(xr_s[...].astype(BF16), cw_ref[cc, 0]) + _dot(xi_s[...].astype(BF16), cw_ref[cc, 1])
        y = y + d_ref[:, lanes] * u
        y_ref[:, lanes] = jax.nn.gelu(y).astype(y_ref.dtype)

    if nt > 1:
        @pl.when(ti == nt - 1)
        def _():
            for cc in range(S5_CHUNKS_PER_STEP):
                xo_ref[0, cc] = jnp.concatenate([car_s[cc, 0, 0:1, :], car_s[cc, 1, 0:1, :]], axis=0)


def _s5_discretize(a_re, a_im, log_step, b_re, b_im, c_re, c_im):
    dt = jnp.exp(log_step)[:, None]
    mag = jnp.exp(dt * a_re)
    ab_re = mag * jnp.cos(dt * a_im)
    ab_im = mag * jnp.sin(dt * a_im)
    den = a_re * a_re + a_im * a_im
    coef_re = ((ab_re - 1.0) * a_re + ab_im * a_im) / den
    coef_im = (ab_im * a_re - (ab_re - 1.0) * a_im) / den
    bb_re = coef_re[..., None] * b_re - coef_im[..., None] * b_im
    bb_im = coef_re[..., None] * b_im + coef_im[..., None] * b_re
    eye = jnp.eye(S5_CHUNK_GROUPS, dtype=F32)

    def blockdiag_b(bb):
        t = bb.reshape(S5_CHUNKS, S5_CHUNK_GROUPS, S5_STATE, S5_GROUP).transpose(0, 1, 3, 2)
        return jnp.einsum('cgxp,gh->cgxhp', t, eye).reshape(S5_CHUNKS, S5_CH, S5_ST)

    def blockdiag_c(cc):
        t = cc.reshape(S5_CHUNKS, S5_CHUNK_GROUPS, S5_GROUP, S5_STATE)
        return jnp.einsum('cgxp,gh->cgphx', t, eye).reshape(S5_CHUNKS, S5_ST, S5_CH)

    bw = jnp.concatenate([blockdiag_b(bb_re), blockdiag_b(bb_im)], axis=2).astype(BF16)
    cw = jnp.stack([blockdiag_c(c_re), -blockdiag_c(c_im)], axis=1).astype(BF16)

    pows = [(ab_re, ab_im)]
    for _ in range(SUBLANES - 1):
        pr, pi = pows[-1]
        pows.append((pr * ab_re - pi * ab_im, pr * ab_im + pi * ab_re))

    def plane(vals):
        return jnp.stack([v.reshape(S5_CHUNKS, S5_ST) for v in vals], axis=1)

    zero = jnp.zeros_like(ab_re)
    planes = []
    for k in (1, 2, 4):
        planes.append(plane([pows[k - 1][0] if t >= k else zero for t in range(SUBLANES)]))
        planes.append(plane([pows[k - 1][1] if t >= k else zero for t in range(SUBLANES)]))
    planes.append(plane([pows[t][0] for t in range(SUBLANES)]))
    planes.append(plane([pows[t][1] for t in range(SUBLANES)]))
    pw = jnp.stack(planes, axis=1)
    return bw, cw, pw


def s5_mixer(proj, x0_re, x0_im, disc, d_skip, *, B, L):
    W = W_BRANCH
    bw, cw, pw = disc
    cps = S5_CHUNKS_PER_STEP
    n_planes = pw.shape[1]
    if L >= 512:
        nseq, seg, nt, bsteps = 1, 512, L // 512, B
    else:
        nseq, seg, nt, bsteps = B, L, 1, 1
    kern = functools.partial(_s5_kernel, nseq=nseq, seg=seg, nt=nt)
    x_scratch = pltpu.VMEM((nseq * seg, S5_ST), F32)
    rows = nseq * seg
    col0 = 5 * W // (cps * S5_CH)
    x0 = jnp.stack([x0_re.reshape(B, S5_CHUNKS, S5_ST), x0_im.reshape(B, S5_CHUNKS, S5_ST)], axis=2)
    state = pl.BlockSpec((nseq, cps, 2, S5_ST), lambda b, c, t: (b, c, 0, 0))
    y, xo = pl.pallas_call(
        kern,
        grid=(bsteps, S5_CHUNKS // cps, nt),
        in_specs=[pl.BlockSpec((rows, cps * S5_CH), lambda b, c, t: (b * nt + t, col0 + c)),
                  pl.BlockSpec((cps, S5_CH, 2 * S5_ST), lambda b, c, t: (c, 0, 0)),
                  pl.BlockSpec((cps, 2, S5_ST, S5_CH), lambda b, c, t: (c, 0, 0, 0)),
                  pl.BlockSpec((cps, n_planes, SUBLANES, S5_ST), lambda b, c, t: (c, 0, 0, 0)),
                  state,
                  pl.BlockSpec((1, cps * S5_CH), lambda b, c, t: (0, c))],
        out_specs=[pl.BlockSpec((rows, cps * S5_CH), lambda b, c, t: (b * nt + t, c)), state],
        out_shape=[jax.ShapeDtypeStruct((B * L, W), BF16),
                   jax.ShapeDtypeStruct((B, S5_CHUNKS, 2, S5_ST), F32)],
        scratch_shapes=[x_scratch] * 4 + [pltpu.VMEM((cps, 2, SUBLANES, S5_ST), F32)],
        compiler_params=_params("parallel", "parallel", "arbitrary"),
        name="s5_scan",
    )(proj, bw, cw, pw, x0, d_skip.reshape(1, W))
    s_re = xo[:, :, 0, :].reshape(B, S5_GROUPS, S5_STATE)
    s_im = xo[:, :, 1, :].reshape(B, S5_GROUPS, S5_STATE)
    return y, s_re, s_im


def _glu_kernel(a_ref, b_ref, y_ref, bias_ref, o_ref):
    a = a_ref[...]
    for cols in _column_pieces(o_ref.shape[1]):
        z = _dot(a, b_ref[:, cols]) + bias_ref[:, cols]
        o_ref[:, cols] = (y_ref[:, cols].astype(F32) * jax.nn.sigmoid(z)).astype(o_ref.dtype)


def s5_glu(y, w_bf16, l, bias, *, tm, tn):
    M, K = y.shape
    N = w_bf16.shape[-1]
    return pl.pallas_call(
        _glu_kernel,
        grid=(M // tm, N // tn),
        in_specs=[pl.BlockSpec((tm, K), lambda i, j: (i, 0)),
                  pl.BlockSpec((None, K, tn), lambda i, j: (l, 0, j)),
                  pl.BlockSpec((tm, tn), lambda i, j: (i, j)),
                  pl.BlockSpec((1, tn), lambda i, j: (0, j))],
        out_specs=pl.BlockSpec((tm, tn), lambda i, j: (i, j)),
        out_shape=jax.ShapeDtypeStruct((M, N), BF16),
        compiler_params=_params("parallel", "arbitrary"),
        name="s5_glu",
    )(y, w_bf16, y, bias.reshape(1, N))


def _merge_kernel(oa_ref, ob_ref, oc_ref, w_ref, ga_ref, gb_ref, gc_ref, o_ref):
    acc = jax.nn.sigmoid(ga_ref[...].astype(F32)) * _dot(oa_ref[...], w_ref[0])
    acc = acc + jax.nn.sigmoid(gb_ref[...].astype(F32)) * _dot(ob_ref[...], w_ref[1])
    acc = acc + jax.nn.sigmoid(gc_ref[...].astype(F32)) * _dot(oc_ref[...], w_ref[2])
    o_ref[...] = acc.astype(o_ref.dtype)


def branch_merge(o_a, o_b, o_c, w_br_bf16, l, gates, *, tm, tn):
    M, W = o_a.shape
    D = D_MODEL
    nj = D // tn
    act = pl.BlockSpec((tm, W), lambda i, j: (i, 0))

    def gate(n):
        return pl.BlockSpec((tm, tn), lambda i, j: (i, n * nj + j))

    return pl.pallas_call(
        _merge_kernel,
        grid=(M // tm, nj),
        in_specs=[act, act, act,
                  pl.BlockSpec((None, N_BRANCH, W, tn), lambda i, j: (l, 0, 0, j)),
                  gate(0), gate(1), gate(2)],
        out_specs=pl.BlockSpec((tm, tn), lambda i, j: (i, j)),
        out_shape=jax.ShapeDtypeStruct((M, D), BF16),
        compiler_params=_params("parallel", "arbitrary"),
        name="branch_merge",
    )(o_a, o_b, o_c, w_br_bf16, gates, gates, gates)


def _conv_gate(ua0, ua1, ua2, uv0, uv1, uv2, cwa_ref, cwv_ref, cba_ref, cbv_ref):
    shape = ua0.shape
    tn = shape[-1]

    def tiles(x):
        return x.reshape(-1, SUBLANES, tn)

    def tap(ref, k):
        return ref[k * SUBLANES:(k + 1) * SUBLANES, :][None]

    a = cba_ref[...][None] + (tap(cwa_ref, 0) * tiles(ua0) + tap(cwa_ref, 1) * tiles(ua1) + tap(cwa_ref, 2) * tiles(ua2))
    v = cbv_ref[...][None] + (tap(cwv_ref, 0) * tiles(uv0) + tap(cwv_ref, 1) * tiles(uv1) + tap(cwv_ref, 2) * tiles(uv2))
    return (jax.nn.gelu(a) * v).reshape(shape)


def _up_conv_long_kernel(x_ref, xp_ref, wal_ref, wah_ref, wvl_ref, wvh_ref, cwa_ref, cwv_ref, cba_ref, cbv_ref,
                         hist_ref, g_ref, tail_ref, xe_s, *, tm, tiles_per_seq):
    i = pl.program_id(0)
    j = pl.program_id(1)
    pad = BF16_ROWS
    half = wal_ref.shape[1]

    @pl.when(j == 0)
    def _():
        xe_s[0:pad, :] = xp_ref[...]
        xe_s[pad:, :] = x_ref[...]

    xe = xe_s[...]
    seq_start = (i % tiles_per_seq) == 0
    use_hist = (lax.broadcasted_iota(jnp.int32, (pad, half), 0) >= pad - 2) & seq_start

    def with_history(u, hist_rows):
        return jnp.concatenate([jnp.where(use_hist, hist_rows, u[:pad]), u[pad:]], axis=0)

    def taps(u):
        return pltpu.roll(u, 2, 0)[pad:], pltpu.roll(u, 1, 0)[pad:], u[pad:]

    def half_tile(c):
        wa_ref, wv_ref = ((wal_ref, wvl_ref), (wah_ref, wvh_ref))[c]
        cols = slice(c * half, (c + 1) * half)
        ua = with_history(_dot(xe, wa_ref[...]), hist_ref[0, 0, :, cols])
        uv = with_history(_dot(xe, wv_ref[...]), hist_ref[1, 0, :, cols])
        g = _conv_gate(*taps(ua), *taps(uv), cwa_ref[:, cols], cwv_ref[:, cols], cba_ref[:, cols], cbv_ref[:, cols])
        g_ref[:, cols] = g.astype(g_ref.dtype)
        tail_ref[0, 0, :, cols] = ua[pad + tm - 2:]
        tail_ref[1, 0, :, cols] = uv[pad + tm - 2:]

    half_tile(0)
    half_tile(1)


def _up_conv_short_kernel(x_ref, wal_ref, wah_ref, wvl_ref, wvh_ref, cwa_ref, cwv_ref, cba_ref, cbv_ref, hist_ref,
                          g_ref, tail_ref, ua_s, uv_s, *, nseq, L):
    x = x_ref[...]
    half = wal_ref.shape[1]
    tn = 2 * half
    pad = SUBLANES
    ua_s[:, pad:, :half] = _dot(x, wal_ref[...]).reshape(nseq, L, half)
    ua_s[:, pad:, half:] = _dot(x, wah_ref[...]).reshape(nseq, L, half)
    uv_s[:, pad:, :half] = _dot(x, wvl_ref[...]).reshape(nseq, L, half)
    uv_s[:, pad:, half:] = _dot(x, wvh_ref[...]).reshape(nseq, L, half)
    ua_s[:, pad - 2:pad, :] = hist_ref[0]
    uv_s[:, pad - 2:pad, :] = hist_ref[1]
    g = _conv_gate(ua_s[:, pad - 2:pad - 2 + L, :], ua_s[:, pad - 1:pad - 1 + L, :], ua_s[:, pad:pad + L, :],
                   uv_s[:, pad - 2:pad - 2 + L, :], uv_s[:, pad - 1:pad - 1 + L, :], uv_s[:, pad:pad + L, :],
                   cwa_ref, cwv_ref, cba_ref, cbv_ref)
    g_ref[...] = g.reshape(nseq * L, tn).astype(g_ref.dtype)
    tail_ref[0] = ua_s[:, pad + L - 2:pad + L, :]
    tail_ref[1] = uv_s[:, pad + L - 2:pad + L, :]


def conv_ffn_up(xn, w_up, l, conv_w, conv_b, hist, *, B, L):
    T, D = xn.shape
    tn = 512
    half = tn // 2
    nj = pl.cdiv(D_FF, tn)
    voff = D_FF // half
    last = 2 * D_FF // half - 1
    cw8 = jnp.repeat(conv_w, SUBLANES, axis=0)
    cb8 = jnp.broadcast_to(conv_b[None, :], (SUBLANES, 2 * D_FF))
    cwa, cwv = cw8[:, :D_FF], cw8[:, D_FF:]
    cba, cbv = cb8[:, :D_FF], cb8[:, D_FF:]
    hist2 = jnp.stack([hist[:, :, :D_FF], hist[:, :, D_FF:]], axis=0)
    if L >= 1024:
        tm = 1024
        tps = L // tm
        nti = T // tm

        def wspec(k):
            return pl.BlockSpec((None, D, half), lambda i, j: (l, 0, jnp.minimum(2 * j + k, last)))

        cwspec = pl.BlockSpec((CONV_W * SUBLANES, tn), lambda i, j: (0, j))
        cbspec = pl.BlockSpec((SUBLANES, tn), lambda i, j: (0, j))
        kern = functools.partial(_up_conv_long_kernel, tm=tm, tiles_per_seq=tps)
        g, tails = pl.pallas_call(
            kern,
            grid=(nti, nj),
            in_specs=[pl.BlockSpec((tm, D), lambda i, j: (i, 0)),
                      pl.BlockSpec((BF16_ROWS, D),
                                   lambda i, j: (jnp.maximum(i * (tm // BF16_ROWS) - 1, 0), 0)),
                      wspec(0), wspec(1), wspec(voff), wspec(voff + 1),
                      cwspec, cwspec, cbspec, cbspec,
                      pl.BlockSpec((2, 1, BF16_ROWS, tn), lambda i, j: (0, i // tps, 0, j))],
            out_specs=[pl.BlockSpec((tm, tn), lambda i, j: (i, j)),
                       pl.BlockSpec((2, 1, 2, tn), lambda i, j: (0, i, 0, j))],
            out_shape=[jax.ShapeDtypeStruct((T, D_FF), BF16),
                       jax.ShapeDtypeStruct((2, nti, 2, D_FF), F32)],
            scratch_shapes=[pltpu.VMEM((tm + BF16_ROWS, D), BF16)],
            compiler_params=_params("parallel", "arbitrary"),
            name="conv_ffn_up_long",
        )(xn, xn, w_up, w_up, w_up, w_up, cwa, cwv, cba, cbv,
          jnp.pad(hist2, ((0, 0), (0, 0), (BF16_ROWS - 2, 0), (0, 0))))
        tails = tails[:, tps - 1::tps]
    else:
        def wspec(k):
            return pl.BlockSpec((None, D, half), lambda j: (l, 0, jnp.minimum(2 * j + k, last)))

        kern = functools.partial(_up_conv_short_kernel, nseq=B, L=L)
        g, tails = pl.pallas_call(
            kern,
            grid=(nj,),
            in_specs=[pl.BlockSpec((T, D), lambda j: (0, 0)),
                      wspec(0), wspec(1), wspec(voff), wspec(voff + 1),
                      pl.BlockSpec((CONV_W * SUBLANES, tn), lambda j: (0, j)),
                      pl.BlockSpec((CONV_W * SUBLANES, tn), lambda j: (0, j)),
                      pl.BlockSpec((SUBLANES, tn), lambda j: (0, j)),
                      pl.BlockSpec((SUBLANES, tn), lambda j: (0, j)),
                      pl.BlockSpec((2, B, 2, tn), lambda j: (0, 0, 0, j))],
            out_specs=[pl.BlockSpec((T, tn), lambda j: (0, j)),
                       pl.BlockSpec((2, B, 2, tn), lambda j: (0, 0, 0, j))],
            out_shape=[jax.ShapeDtypeStruct((T, D_FF), BF16),
                       jax.ShapeDtypeStruct((2, B, 2, D_FF), F32)],
            scratch_shapes=[pltpu.VMEM((B, L + SUBLANES, tn), F32),
                            pltpu.VMEM((B, L + SUBLANES, tn), F32)],
            compiler_params=_params("arbitrary"),
            name="conv_ffn_up_short",
        )(xn, w_up, w_up, w_up, w_up, cwa, cwv, cba, cbv, hist2)
    s_conv = jnp.concatenate([tails[0], tails[1]], axis=-1)
    return g, s_conv


def _ple_kernel(x_ref, wg_ref, p_ref, wp_ref, h_ref, o_ref):
    x = x_ref[...]
    pb = p_ref[...].astype(BF16)
    for cols in _column_pieces(o_ref.shape[1]):
        gate = jax.nn.sigmoid(_dot(x, wg_ref[:, cols]))
        emb = _dot(pb, wp_ref[:, cols])
        o_ref[:, cols] = h_ref[:, cols] + gate * emb


def ple_update(xn, w_pg, p, w_ple, l, h, *, tm, tn):
    M, D = xn.shape
    return pl.pallas_call(
        _ple_kernel,
        grid=(M // tm, D // tn),
        in_specs=[pl.BlockSpec((tm, D), lambda i, j: (i, 0)),
                  pl.BlockSpec((None, D, tn), lambda i, j: (l, 0, j)),
                  pl.BlockSpec((tm, PLE_DIM), lambda i, j: (i, 0)),
                  pl.BlockSpec((None, PLE_DIM, tn), lambda i, j: (l, 0, j)),
                  pl.BlockSpec((tm, tn), lambda i, j: (i, j))],
        out_specs=pl.BlockSpec((tm, tn), lambda i, j: (i, j)),
        out_shape=jax.ShapeDtypeStruct((M, D), F32),
        input_output_aliases={4: 0},
        compiler_params=_params("parallel", "arbitrary"),
        name="ple_update",
    )(xn, w_pg, p, w_ple, h)


def _layer_stack(x, p, st_pool, st_hg, st_s5r, st_s5i, st_conv, pos0, lb, wts):
    B, L, D = x.shape
    T = B * L
    W = W_BRANCH
    tm = min(T, 1024)
    h = x.reshape(T, D)
    n_pool, n_hg, n_s5r, n_s5i, n_conv = [], [], [], [], []
    w = wts
    for l in range(DEPTH):
        xn = rms_norm(h, w['norm_mix'][l], BF16)
        proj = matmul(xn, w['w_in'], l, F32, tm=tm, tn=1024, col_block0=0, n_cols=N_MIX, name="in_proj_mix")
        gates = matmul(xn, w['w_in'], l, BF16, tm=tm, tn=1024, col_block0=N_MIX // 1024,
                       n_cols=N_BRANCH * D, name="in_proj_gates")
        o_a, s_pool = pool_mixer(proj, st_pool[l], w['pool_w'], l, w['pool_scale'][l], B=B, L=L, pos0=pos0)
        o_b, s_hg = hgrn2_mixer(proj, lb[l], w['hg_norm'][l], st_hg[l], B=B, L=L)
        y_c, s_r, s_i = s5_mixer(proj, st_s5r[l], st_s5i[l], w['s5_disc'][l], w['s5_d'][l], B=B, L=L)
        o_c = s5_glu(y_c, w['s5_w_glu'], l, w['s5_b_glu'][l], tm=tm, tn=1024)
        merged = branch_merge(o_a, o_b, o_c, w['w_br'], l, gates, tm=tm, tn=512)
        h = matmul_residual(merged, w['w_out'], l, h, tm=tm, tn=1024, name="out_proj", in_place=l > 0)
        xn = rms_norm(h, w['norm_ffn'][l], BF16)
        g, s_conv = conv_ffn_up(xn, w['w_up'], l, w['conv_w'][l], w['conv_b'][l], st_conv[l], B=B, L=L)
        h = matmul_residual(g, w['w_down'], l, h, tm=min(T, 512), tn=512, name="ffn_down")
        xn = rms_norm(h, w['norm_ple'][l], BF16)
        h = ple_update(xn, w['w_ple_gate'], p[l].reshape(T, PLE_DIM), w['w_ple'], l, h, tm=tm, tn=512)
        n_pool.append(s_pool)
        n_hg.append(s_hg)
        n_s5r.append(s_r)
        n_s5i.append(s_i)
        n_conv.append(s_conv)
    y = rms_norm(h, w['norm_final'], F32).reshape(B, L, D)
    return (y, jnp.stack(n_pool), jnp.stack(n_hg), jnp.stack(n_s5r), jnp.stack(n_s5i), jnp.stack(n_conv))


def kernel(x_prompt, x_sample, state_pool, state_hgrn, state_s5_re, state_s5_im, state_ffn_conv,
           p_prompt, p_sample, norm_mix, w_in, pool_w, pool_scale, hg_lb_logits, hg_norm,
           s5_a_re, s5_a_im, s5_log_step, s5_b_re, s5_b_im, s5_c_re, s5_c_im, s5_d, s5_w_glu, s5_b_glu,
           w_br, w_out, norm_ffn, w_up, conv_w, conv_b, w_down, norm_ple, w_ple_gate, w_ple, norm_final):
    lb = forget_lower_bounds(hg_lb_logits)
    wts = {
        'norm_mix': norm_mix, 'w_in': cast_bf16(w_in), 'pool_w': cast_bf16(pool_w),
        'pool_scale': pool_scale, 'hg_norm': hg_norm,
        's5_disc': [_s5_discretize(s5_a_re[l], s5_a_im[l], s5_log_step[l], s5_b_re[l], s5_b_im[l],
                                   s5_c_re[l], s5_c_im[l]) for l in range(DEPTH)],
        's5_d': s5_d, 's5_w_glu': cast_bf16(s5_w_glu), 's5_b_glu': s5_b_glu,
        'w_br': cast_bf16(w_br), 'w_out': cast_bf16(w_out), 'norm_ffn': norm_ffn,
        'w_up': cast_bf16(w_up), 'conv_w': conv_w, 'conv_b': conv_b, 'w_down': cast_bf16(w_down),
        'norm_ple': norm_ple, 'w_ple_gate': cast_bf16(w_ple_gate), 'w_ple': cast_bf16(w_ple),
        'norm_final': norm_final,
    }
    bp = x_prompt.shape[0]
    z_pool = jnp.zeros((DEPTH, bp) + state_pool.shape[2:], F32)
    z_hg = jnp.zeros((DEPTH, bp) + state_hgrn.shape[2:], F32)
    z_s5 = jnp.zeros((DEPTH, bp) + state_s5_re.shape[2:], F32)
    z_conv = jnp.zeros((DEPTH, bp) + state_ffn_conv.shape[2:], F32)
    y_p, pool_p, hg_p, s5r_p, s5i_p, conv_p = _layer_stack(
        x_prompt, p_prompt, z_pool, z_hg, z_s5, z_s5, z_conv, 0, lb, wts)
    y_s, pool_s, hg_s, s5r_s, s5i_s, conv_s = _layer_stack(
        x_sample, p_sample, state_pool, state_hgrn, state_s5_re, state_s5_im, state_ffn_conv,
        PAST_LEN, lb, wts)
    return (y_p, y_s, pool_p, hg_p, s5r_p, s5i_p, conv_p, pool_s, hg_s, s5r_s, s5i_s, conv_s)
```

```python
import functools

import jax
import jax.numpy as jnp
from jax import lax
from jax.experimental import pallas as pl
from jax.experimental.pallas import tpu as pltpu

D_MODEL = 4096
DEPTH = 4
PAST_LEN = 1024
W_BRANCH = D_MODEL // 2
N_BRANCH = 3
POOL_WINDOWS = (2, 4, 8, 16)
POOL_GROUP = W_BRANCH // len(POOL_WINDOWS)
POOL_HIST = max(POOL_WINDOWS) - 1
HG_DK = 128
HG_HEADS = W_BRANCH // HG_DK
HG_DV = W_BRANCH // HG_HEADS
HG_BLOCK = 16
S5_GROUP = 16
S5_GROUPS = W_BRANCH // S5_GROUP
S5_STATE = 64
D_FF = 11008
CONV_W = 3
PLE_DIM = 256
N_MIX = 6 * W_BRANCH
N_IN = N_MIX + N_BRANCH * D_MODEL
EPS = 1e-6

F32 = jnp.float32
BF16 = jnp.bfloat16

VMEM_LIMIT_BYTES = 56 * 1024 * 1024
SUBLANES = 8
BF16_ROWS = 16

S5_CHUNK_GROUPS = 16
S5_CHUNKS = S5_GROUPS // S5_CHUNK_GROUPS
S5_CH = S5_CHUNK_GROUPS * S5_GROUP
S5_ST = S5_CHUNK_GROUPS * S5_STATE


def _params(*sem, flags=None):
    return pltpu.CompilerParams(dimension_semantics=sem, vmem_limit_bytes=VMEM_LIMIT_BYTES, flags=flags)


def _dot(a, b):
    return jnp.dot(a, b, preferred_element_type=F32)


MXU_COLUMNS = 256


def _column_pieces(n, width=MXU_COLUMNS):
    return [slice(c, c + width) for c in range(0, n, width)]


def _norm_kernel(x_ref, g_ref, o_ref):
    x = x_ref[...]
    ms = jnp.mean(x * x, axis=-1, keepdims=True)
    o_ref[...] = ((x * lax.rsqrt(ms + EPS)) * g_ref[...]).astype(o_ref.dtype)


def rms_norm(x, g, out_dtype):
    T, D = x.shape
    tr = min(T, 512)
    return pl.pallas_call(
        _norm_kernel,
        grid=(T // tr,),
        in_specs=[pl.BlockSpec((tr, D), lambda i: (i, 0)),
                  pl.BlockSpec((1, D), lambda i: (0, 0))],
        out_specs=pl.BlockSpec((tr, D), lambda i: (i, 0)),
        out_shape=jax.ShapeDtypeStruct((T, D), out_dtype),
        compiler_params=_params("parallel"),
        name="rms_norm",
    )(x, g.reshape(1, D))


def _cast_kernel(x_ref, o_ref):
    o_ref[...] = x_ref[...].astype(o_ref.dtype)


def cast_bf16(w):
    shape = w.shape
    N = shape[-1]
    R = w.size // N
    tr = 1024
    tc = min(N, 2048)
    out = pl.pallas_call(
        _cast_kernel,
        grid=(R // tr, pl.cdiv(N, tc)),
        in_specs=[pl.BlockSpec((tr, tc), lambda i, j: (i, j))],
        out_specs=pl.BlockSpec((tr, tc), lambda i, j: (i, j)),
        out_shape=jax.ShapeDtypeStruct((R, N), BF16),
        compiler_params=_params("parallel", "parallel"),
        name="cast_bf16",
    )(w.reshape(R, N))
    return out.reshape(shape)


def _mm_kernel(a_ref, b_ref, o_ref):
    o_ref[...] = _dot(a_ref[...], b_ref[...]).astype(o_ref.dtype)


def matmul(a, b, l, out_dtype, *, tm, tn, col_block0, n_cols, name):
    M, K = a.shape
    return pl.pallas_call(
        _mm_kernel,
        grid=(M // tm, n_cols // tn),
        in_specs=[pl.BlockSpec((tm, K), lambda i, j: (i, 0)),
                  pl.BlockSpec((None, K, tn), lambda i, j: (l, 0, j + col_block0))],
        out_specs=pl.BlockSpec((tm, tn), lambda i, j: (i, j)),
        out_shape=jax.ShapeDtypeStruct((M, n_cols), out_dtype),
        compiler_params=_params("parallel", "arbitrary"),
        name=name,
    )(a, b)


def _mm_res_kernel(a_ref, b_ref, h_ref, o_ref):
    o_ref[...] = h_ref[...] + _dot(a_ref[...], b_ref[...])


def matmul_residual(a, b, l, h, *, tm, tn, name, in_place=True):
    M, K = a.shape
    N = b.shape[-1]
    return pl.pallas_call(
        _mm_res_kernel,
        grid=(M // tm, N // tn),
        in_specs=[pl.BlockSpec((tm, K), lambda i, j: (i, 0)),
                  pl.BlockSpec((None, K, tn), lambda i, j: (l, 0, j)),
                  pl.BlockSpec((tm, tn), lambda i, j: (i, j))],
        out_specs=pl.BlockSpec((tm, tn), lambda i, j: (i, j)),
        out_shape=jax.ShapeDtypeStruct((M, N), F32),
        input_output_aliases={2: 0} if in_place else {},
        compiler_params=_params("parallel", "arbitrary"),
        name=name,
    )(a, b, h)


def _lb_kernel(x_ref, o_ref):
    x = x_ref[...]
    m = jnp.max(x, axis=0, keepdims=True)
    e = jnp.exp(x - m)
    p = e / jnp.sum(e, axis=0, keepdims=True)
    rows = [p[0:1]]
    for l in range(1, DEPTH):
        rows.append(rows[-1] + p[l:l + 1])
    o_ref[...] = jnp.concatenate([r - rows[0] for r in rows], axis=0)


def forget_lower_bounds(logits):
    return pl.pallas_call(
        _lb_kernel,
        out_shape=jax.ShapeDtypeStruct(logits.shape, F32),
        name="hgrn_lower_bounds",
    )(logits)


def _pool_kernel(u_ref, prev_ref, hist_ref, w_ref, scale_ref, o_ref, st_ref, ext_ref, *, tt, pos0):
    ti = pl.program_id(1)
    nt = pl.num_programs(1)
    hist_rows = POOL_HIST + 1

    @pl.when(ti == 0)
    def _():
        ext_ref[0:hist_rows, :] = hist_ref[0]

    @pl.when(ti > 0)
    def _():
        ext_ref[0:hist_rows, :] = prev_ref[...]

    cur = u_ref[...]
    ext_ref[hist_rows:, :] = cur
    pos = pos0 + ti * tt + lax.broadcasted_iota(jnp.int32, (tt, 1), 0)
    for gi, w in enumerate(POOL_WINDOWS):
        cols = slice(gi * POOL_GROUP, (gi + 1) * POOL_GROUP)
        win = ext_ref[:, cols]
        shift = 1
        while shift < w:
            win = win + pltpu.roll(win, shift, 0)
            shift *= 2
        win = win[hist_rows:]
        cnt = jnp.minimum(pos + 1, w).astype(F32)
        pooled = win / cnt - cur[:, cols]
        y = _dot(pooled.astype(BF16), w_ref[gi]) * scale_ref[:, cols]
        o_ref[:, cols] = y.astype(o_ref.dtype)

    @pl.when(ti == nt - 1)
    def _():
        st_ref[0] = ext_ref[pl.ds(tt + 1, POOL_HIST), :]


def pool_mixer(proj, hist, w_bf16, l, scale, *, B, L, pos0):
    W = W_BRANCH
    tt = min(L, 256)
    nt = L // tt
    hist16 = jnp.concatenate([jnp.zeros((B, 1, W), F32), hist], axis=1)
    blk16 = tt // (POOL_HIST + 1)
    kern = functools.partial(_pool_kernel, tt=tt, pos0=pos0)
    return pl.pallas_call(
        kern,
        grid=(B, nt),
        in_specs=[pl.BlockSpec((tt, W), lambda b, t: (b * nt + t, 0)),
                  pl.BlockSpec((POOL_HIST + 1, W),
                               lambda b, t: (jnp.maximum((b * nt + t) * blk16 - 1, 0), 0)),
                  pl.BlockSpec((1, POOL_HIST + 1, W), lambda b, t: (b, 0, 0)),
                  pl.BlockSpec((None, len(POOL_WINDOWS), POOL_GROUP, POOL_GROUP), lambda b, t: (l, 0, 0, 0)),
                  pl.BlockSpec((1, W), lambda b, t: (0, 0))],
        out_specs=[pl.BlockSpec((tt, W), lambda b, t: (b * nt + t, 0)),
                   pl.BlockSpec((1, POOL_HIST, W), lambda b, t: (b, 0, 0))],
        out_shape=[jax.ShapeDtypeStruct((B * L, W), BF16),
                   jax.ShapeDtypeStruct((B, POOL_HIST, W), F32)],
        scratch_shapes=[pltpu.VMEM((tt + POOL_HIST + 1, W), F32)],
        compiler_params=_params("parallel", "arbitrary"),
        name="pool_mixer",
    )(proj, proj, hist16, w_bf16, scale.reshape(1, W))


HG_HEADS_PER_STEP = 16


def _hgrn_kernel(q_ref, f_ref, v_ref, g_ref, lb_ref, ng_ref, s0_ref, o_ref, s_ref, st_ref, oacc_ref, *, C):
    ci = pl.program_id(2)
    nc = pl.num_programs(2)
    hb = HG_HEADS_PER_STEP
    nsub = C // HG_BLOCK
    ref_row = HG_BLOCK // 2 - 1

    @pl.when(ci == 0)
    def _():
        for h in range(hb):
            st_ref[h] = s0_ref[0, h].T

    lb = lb_ref[...]
    f = lb + (1.0 - lb) * jax.nn.sigmoid(f_ref[...])
    logf = jnp.log(f)
    kk = 1.0 - f
    tri = (lax.broadcasted_iota(jnp.int32, (C, C), 1)
           <= lax.broadcasted_iota(jnp.int32, (C, C), 0)).astype(BF16)
    w = logf.shape[1]
    hi = logf.astype(BF16)
    rest = logf - hi.astype(F32)
    mid = rest.astype(BF16)
    lo = (rest - mid.astype(F32)).astype(BF16)
    parts = _dot(tri, jnp.concatenate([hi, mid, lo], axis=1))
    bcum = parts[:, :w] + (parts[:, w:2 * w] + parts[:, 2 * w:])
    blast = bcum[C - 1:C, :]
    q = q_ref[...]
    v = v_ref[...]
    qd = (q * jnp.exp(bcum)).astype(BF16)
    kdec = (kk * jnp.exp(blast - bcum)).astype(BF16)
    eb = jnp.exp(blast)
    vb = v.astype(BF16)
    gate = jax.nn.silu(g_ref[...])
    ng = ng_ref[...]
    for h in range(hb):
        sl = slice(h * HG_DK, (h + 1) * HG_DK)
        st = st_ref[h]
        oacc_ref[...] = lax.dot_general(qd[:, sl], st.astype(BF16), (((1,), (1,)), ((), ())),
                                        preferred_element_type=F32)
        for j in range(nsub):
            r0 = j * HG_BLOCK
            bj = bcum[r0:, sl]
            bref = bcum[r0 + ref_row:r0 + ref_row + 1, sl]
            qj = (q[r0:, sl] * jnp.exp(bj - bref)).astype(BF16)
            kj = (kk[r0:r0 + HG_BLOCK, sl] * jnp.exp(bref - bj[:HG_BLOCK])).astype(BF16)
            att = lax.dot_general(qj, kj, (((1,), (1,)), ((), ())), preferred_element_type=F32)
            rr = lax.broadcasted_iota(jnp.int32, att.shape, 0)
            cc = lax.broadcasted_iota(jnp.int32, att.shape, 1)
            att = jnp.where(cc <= rr, att, 0.0)
            oacc_ref[r0:, :] += _dot(att.astype(BF16), vb[r0:r0 + HG_BLOCK, sl])
        o = oacc_ref[...]
        o = o * lax.rsqrt(jnp.mean(o * o, axis=-1, keepdims=True) + EPS)
        o = o * ng[:, sl]
        o_ref[:, sl] = (o * gate[:, sl]).astype(o_ref.dtype)
        upd = lax.dot_general(vb[:, sl], kdec[:, sl], (((0,), (0,)), ((), ())),
                              preferred_element_type=F32)
        st_ref[h] = eb[:, sl] * st + upd

    @pl.when(ci == nc - 1)
    def _():
        for h in range(hb):
            s_ref[0, h] = st_ref[h].T


def hgrn2_mixer(proj, lb, norm_g, s0, *, B, L):
    W = W_BRANCH
    C = min(L, 128)
    nc = L // C
    hb = HG_HEADS_PER_STEP
    nh = HG_HEADS // hb
    wb = hb * HG_DK
    per = W // wb

    def col(k):
        return pl.BlockSpec((C, wb), lambda b, h, c: (b * nc + c, k * per + h))

    vec = pl.BlockSpec((1, wb), lambda b, h, c: (0, h))
    st = pl.BlockSpec((1, hb, HG_DK, HG_DV), lambda b, h, c: (b, h, 0, 0))
    kern = functools.partial(_hgrn_kernel, C=C)
    return pl.pallas_call(
        kern,
        grid=(B, nh, nc),
        in_specs=[col(1), col(2), col(3), col(4), vec, vec, st],
        out_specs=[pl.BlockSpec((C, wb), lambda b, h, c: (b * nc + c, h)), st],
        out_shape=[jax.ShapeDtypeStruct((B * L, W), BF16),
                   jax.ShapeDtypeStruct((B, HG_HEADS, HG_DK, HG_DV), F32)],
        scratch_shapes=[pltpu.VMEM((hb, HG_DV, HG_DK), F32),
                        pltpu.VMEM((C, HG_DV), F32)],
        compiler_params=_params("parallel", "parallel", "arbitrary"),
        name="hgrn2_mixer",
    )(proj, proj, proj, proj, lb.reshape(1, W), norm_g.reshape(1, W), s0)


S5_CHUNKS_PER_STEP = 4


def _s5_scan_tile(xr, xi, cr, cim, pw_ref, cc):
    for idx, k in enumerate((1, 2, 4)):
        sr = pltpu.roll(xr, k, 0)
        si = pltpu.roll(xi, k, 0)
        mr = pw_ref[cc, 2 * idx]
        mi = pw_ref[cc, 2 * idx + 1]
        xr, xi = xr + (mr * sr - mi * si), xi + (mr * si + mi * sr)
    pr = pw_ref[cc, 6]
    pim = pw_ref[cc, 7]
    return xr + (pr * cr - pim * cim), xi + (pr * cim + pim * cr)


def _s5_kernel(u_ref, bw_ref, cw_ref, pw_ref, x0_ref, d_ref, y_ref, xo_ref,
               xr0_s, xi0_s, xr1_s, xi1_s, car_s, *, nseq, seg, nt):
    ti = pl.program_id(2)
    bufs = ((xr0_s, xi0_s), (xr1_s, xi1_s))

    def bcast(row):
        return jnp.broadcast_to(row, (SUBLANES, S5_ST))

    if nt > 1:
        @pl.when(ti == 0)
        def _():
            for cc in range(S5_CHUNKS_PER_STEP):
                car_s[cc, 0] = bcast(x0_ref[0, cc, 0:1, :])
                car_s[cc, 1] = bcast(x0_ref[0, cc, 1:2, :])

    for cc in range(S5_CHUNKS_PER_STEP):
        xr_s, xi_s = bufs[cc % 2]
        lanes = slice(cc * S5_CH, (cc + 1) * S5_CH)
        u = u_ref[:, lanes]
        bu = _dot(u.astype(BF16), bw_ref[cc])
        xr_s[...] = bu[:, :S5_ST]
        xi_s[...] = bu[:, S5_ST:]
        for q in range(nseq):
            if nt > 1:
                cr, cim = car_s[cc, 0], car_s[cc, 1]
            else:
                cr, cim = bcast(x0_ref[q, cc, 0:1, :]), bcast(x0_ref[q, cc, 1:2, :])
            for r in range(seg // SUBLANES):
                rows = slice(q * seg + r * SUBLANES, q * seg + (r + 1) * SUBLANES)
                xr, xi = _s5_scan_tile(xr_s[rows, :], xi_s[rows, :], cr, cim, pw_ref, cc)
                xr_s[rows, :] = xr
                xi_s[rows, :] = xi
                cr, cim = bcast(xr[SUBLANES - 1:SUBLANES]), bcast(xi[SUBLANES - 1:SUBLANES])
            if nt > 1:
                car_s[cc, 0] = cr
                car_s[cc, 1] = cim
            else:
                xo_ref[q, cc] = jnp.concatenate([cr[0:1], cim[0:1]], axis=0)
        y = _dot(xr_s[...].astype(BF16), cw_ref[cc, 0]) + _dot(xi_s[...].astype(BF16), cw_ref[cc, 1])
        y = y + d_ref[:, lanes] * u
        y_ref[:, lanes] = jax.nn.gelu(y).astype(y_ref.dtype)

    if nt > 1:
        @pl.when(ti == nt - 1)
        def _():
            for cc in range(S5_CHUNKS_PER_STEP):
                xo_ref[0, cc] = jnp.concatenate([car_s[cc, 0, 0:1, :], car_s[cc, 1, 0:1, :]], axis=0)


def _s5_discretize(a_re, a_im, log_step, b_re, b_im, c_re, c_im):
    dt = jnp.exp(log_step)[:, None]
    mag = jnp.exp(dt * a_re)
    ab_re = mag * jnp.cos(dt * a_im)
    ab_im = mag * jnp.sin(dt * a_im)
    den = a_re * a_re + a_im * a_im
    coef_re = ((ab_re - 1.0) * a_re + ab_im * a_im) / den
    coef_im = (ab_im * a_re - (ab_re - 1.0) * a_im) / den
    bb_re = coef_re[..., None] * b_re - coef_im[..., None] * b_im
    bb_im = coef_re[..., None] * b_im + coef_im[..., None] * b_re
    eye = jnp.eye(S5_CHUNK_GROUPS, dtype=F32)

    def blockdiag_b(bb):
        t = bb.reshape(S5_CHUNKS, S5_CHUNK_GROUPS, S5_STATE, S5_GROUP).transpose(0, 1, 3, 2)
        return jnp.einsum('cgxp,gh->cgxhp', t, eye).reshape(S5_CHUNKS, S5_CH, S5_ST)

    def blockdiag_c(cc):
        t = cc.reshape(S5_CHUNKS, S5_CHUNK_GROUPS, S5_GROUP, S5_STATE)
        return jnp.einsum('cgxp,gh->cgphx', t, eye).reshape(S5_CHUNKS, S5_ST, S5_CH)

    bw = jnp.concatenate([blockdiag_b(bb_re), blockdiag_b(bb_im)], axis=2).astype(BF16)
    cw = jnp.stack([blockdiag_c(c_re), -blockdiag_c(c_im)], axis=1).astype(BF16)

    pows = [(ab_re, ab_im)]
    for _ in range(SUBLANES - 1):
        pr, pi = pows[-1]
        pows.append((pr * ab_re - pi * ab_im, pr * ab_im + pi * ab_re))

    def plane(vals):
        return jnp.stack([v.reshape(S5_CHUNKS, S5_ST) for v in vals], axis=1)

    zero = jnp.zeros_like(ab_re)
    planes = []
    for k in (1, 2, 4):
        planes.append(plane([pows[k - 1][0] if t >= k else zero for t in range(SUBLANES)]))
        planes.append(plane([pows[k - 1][1] if t >= k else zero for t in range(SUBLANES)]))
    planes.append(plane([pows[t][0] for t in range(SUBLANES)]))
    planes.append(plane([pows[t][1] for t in range(SUBLANES)]))
    pw = jnp.stack(planes, axis=1)
    return bw, cw, pw


def s5_mixer(proj, x0_re, x0_im, disc, d_skip, *, B, L):
    W = W_BRANCH
    bw, cw, pw = disc
    cps = S5_CHUNKS_PER_STEP
    n_planes = pw.shape[1]
    if L >= 512:
        nseq, seg, nt, bsteps = 1, 512, L // 512, B
    else:
        nseq, seg, nt, bsteps = B, L, 1, 1
    kern = functools.partial(_s5_kernel, nseq=nseq, seg=seg, nt=nt)
    x_scratch = pltpu.VMEM((nseq * seg, S5_ST), F32)
    rows = nseq * seg
    col0 = 5 * W // (cps * S5_CH)
    x0 = jnp.stack([x0_re.reshape(B, S5_CHUNKS, S5_ST), x0_im.reshape(B, S5_CHUNKS, S5_ST)], axis=2)
    state = pl.BlockSpec((nseq, cps, 2, S5_ST), lambda b, c, t: (b, c, 0, 0))
    y, xo = pl.pallas_call(
        kern,
        grid=(bsteps, S5_CHUNKS // cps, nt),
        in_specs=[pl.BlockSpec((rows, cps * S5_CH), lambda b, c, t: (b * nt + t, col0 + c)),
                  pl.BlockSpec((cps, S5_CH, 2 * S5_ST), lambda b, c, t: (c, 0, 0)),
                  pl.BlockSpec((cps, 2, S5_ST, S5_CH), lambda b, c, t: (c, 0, 0, 0)),
                  pl.BlockSpec((cps, n_planes, SUBLANES, S5_ST), lambda b, c, t: (c, 0, 0, 0)),
                  state,
                  pl.BlockSpec((1, cps * S5_CH), lambda b, c, t: (0, c))],
        out_specs=[pl.BlockSpec((rows, cps * S5_CH), lambda b, c, t: (b * nt + t, c)), state],
        out_shape=[jax.ShapeDtypeStruct((B * L, W), BF16),
                   jax.ShapeDtypeStruct((B, S5_CHUNKS, 2, S5_ST), F32)],
        scratch_shapes=[x_scratch] * 4 + [pltpu.VMEM((cps, 2, SUBLANES, S5_ST), F32)],
        compiler_params=_params("parallel", "parallel", "arbitrary"),
        name="s5_scan",
    )(proj, bw, cw, pw, x0, d_skip.reshape(1, W))
    s_re = xo[:, :, 0, :].reshape(B, S5_GROUPS, S5_STATE)
    s_im = xo[:, :, 1, :].reshape(B, S5_GROUPS, S5_STATE)
    return y, s_re, s_im


def _glu_kernel(a_ref, b_ref, y_ref, bias_ref, o_ref):
    a = a_ref[...]
    for cols in _column_pieces(o_ref.shape[1]):
        z = _dot(a, b_ref[:, cols]) + bias_ref[:, cols]
        o_ref[:, cols] = (y_ref[:, cols].astype(F32) * jax.nn.sigmoid(z)).astype(o_ref.dtype)


def s5_glu(y, w_bf16, l, bias, *, tm, tn):
    M, K = y.shape
    N = w_bf16.shape[-1]
    return pl.pallas_call(
        _glu_kernel,
        grid=(M // tm, N // tn),
        in_specs=[pl.BlockSpec((tm, K), lambda i, j: (i, 0)),
                  pl.BlockSpec((None, K, tn), lambda i, j: (l, 0, j)),
                  pl.BlockSpec((tm, tn), lambda i, j: (i, j)),
                  pl.BlockSpec((1, tn), lambda i, j: (0, j))],
        out_specs=pl.BlockSpec((tm, tn), lambda i, j: (i, j)),
        out_shape=jax.ShapeDtypeStruct((M, N), BF16),
        compiler_params=_params("parallel", "arbitrary"),
        name="s5_glu",
    )(y, w_bf16, y, bias.reshape(1, N))


def _merge_kernel(oa_ref, ob_ref, oc_ref, w_ref, ga_ref, gb_ref, gc_ref, o_ref):
    acc = jax.nn.sigmoid(ga_ref[...].astype(F32)) * _dot(oa_ref[...], w_ref[0])
    acc = acc + jax.nn.sigmoid(gb_ref[...].astype(F32)) * _dot(ob_ref[...], w_ref[1])
    acc = acc + jax.nn.sigmoid(gc_ref[...].astype(F32)) * _dot(oc_ref[...], w_ref[2])
    o_ref[...] = acc.astype(o_ref.dtype)


def branch_merge(o_a, o_b, o_c, w_br_bf16, l, gates, *, tm, tn):
    M, W = o_a.shape
    D = D_MODEL
    nj = D // tn
    act = pl.BlockSpec((tm, W), lambda i, j: (i, 0))

    def gate(n):
        return pl.BlockSpec((tm, tn), lambda i, j: (i, n * nj + j))

    return pl.pallas_call(
        _merge_kernel,
        grid=(M // tm, nj),
        in_specs=[act, act, act,
                  pl.BlockSpec((None, N_BRANCH, W, tn), lambda i, j: (l, 0, 0, j)),
                  gate(0), gate(1), gate(2)],
        out_specs=pl.BlockSpec((tm, tn), lambda i, j: (i, j)),
        out_shape=jax.ShapeDtypeStruct((M, D), BF16),
        compiler_params=_params("parallel", "arbitrary"),
        name="branch_merge",
    )(o_a, o_b, o_c, w_br_bf16, gates, gates, gates)


def _conv_gate(ua0, ua1, ua2, uv0, uv1, uv2, cwa_ref, cwv_ref, cba_ref, cbv_ref):
    shape = ua0.shape
    tn = shape[-1]

    def tiles(x):
        return x.reshape(-1, SUBLANES, tn)

    def tap(ref, k):
        return ref[k * SUBLANES:(k + 1) * SUBLANES, :][None]

    a = cba_ref[...][None] + (tap(cwa_ref, 0) * tiles(ua0) + tap(cwa_ref, 1) * tiles(ua1) + tap(cwa_ref, 2) * tiles(ua2))
    v = cbv_ref[...][None] + (tap(cwv_ref, 0) * tiles(uv0) + tap(cwv_ref, 1) * tiles(uv1) + tap(cwv_ref, 2) * tiles(uv2))
    return (jax.nn.gelu(a) * v).reshape(shape)


def _up_conv_long_kernel(x_ref, xp_ref, wal_ref, wah_ref, wvl_ref, wvh_ref, cwa_ref, cwv_ref, cba_ref, cbv_ref,
                         hist_ref, g_ref, tail_ref, xe_s, *, tm, tiles_per_seq):
    i = pl.program_id(0)
    j = pl.program_id(1)
    pad = BF16_ROWS
    half = wal_ref.shape[1]

    @pl.when(j == 0)
    def _():
        xe_s[0:pad, :] = xp_ref[...]
        xe_s[pad:, :] = x_ref[...]

    xe = xe_s[...]
    seq_start = (i % tiles_per_seq) == 0
    use_hist = (lax.broadcasted_iota(jnp.int32, (pad, half), 0) >= pad - 2) & seq_start

    def with_history(u, hist_rows):
        return jnp.concatenate([jnp.where(use_hist, hist_rows, u[:pad]), u[pad:]], axis=0)

    def taps(u):
        return pltpu.roll(u, 2, 0)[pad:], pltpu.roll(u, 1, 0)[pad:], u[pad:]

    def half_tile(c):
        wa_ref, wv_ref = ((wal_ref, wvl_ref), (wah_ref, wvh_ref))[c]
        cols = slice(c * half, (c + 1) * half)
        ua = with_history(_dot(xe, wa_ref[...]), hist_ref[0, 0, :, cols])
        uv = with_history(_dot(xe, wv_ref[...]), hist_ref[1, 0, :, cols])
        g = _conv_gate(*taps(ua), *taps(uv), cwa_ref[:, cols], cwv_ref[:, cols], cba_ref[:, cols], cbv_ref[:, cols])
        g_ref[:, cols] = g.astype(g_ref.dtype)
        tail_ref[0, 0, :, cols] = ua[pad + tm - 2:]
        tail_ref[1, 0, :, cols] = uv[pad + tm - 2:]

    half_tile(0)
    half_tile(1)


def _up_conv_short_kernel(x_ref, wal_ref, wah_ref, wvl_ref, wvh_ref, cwa_ref, cwv_ref, cba_ref, cbv_ref, hist_ref,
                          g_ref, tail_ref, ua_s, uv_s, *, nseq, L):
    x = x_ref[...]
    half = wal_ref.shape[1]
    tn = 2 * half
    pad = SUBLANES
    ua_s[:, pad:, :half] = _dot(x, wal_ref[...]).reshape(nseq, L, half)
    ua_s[:, pad:, half:] = _dot(x, wah_ref[...]).reshape(nseq, L, half)
    uv_s[:, pad:, :half] = _dot(x, wvl_ref[...]).reshape(nseq, L, half)
    uv_s[:, pad:, half:] = _dot(x, wvh_ref[...]).reshape(nseq, L, half)
    ua_s[:, pad - 2:pad, :] = hist_ref[0]
    uv_s[:, pad - 2:pad, :] = hist_ref[1]
    g = _conv_gate(ua_s[:, pad - 2:pad - 2 + L, :], ua_s[:, pad - 1:pad - 1 + L, :], ua_s[:, pad:pad + L, :],
                   uv_s[:, pad - 2:pad - 2 + L, :], uv_s[:, pad - 1:pad - 1 + L, :], uv_s[:, pad:pad + L, :],
                   cwa_ref, cwv_ref, cba_ref, cbv_ref)
    g_ref[...] = g.reshape(nseq * L, tn).astype(g_ref.dtype)
    tail_ref[0] = ua_s[:, pad + L - 2:pad + L, :]
    tail_ref[1] = uv_s[:, pad + L - 2:pad + L, :]


def conv_ffn_up(xn, w_up, l, conv_w, conv_b, hist, *, B, L):
    T, D = xn.shape
    tn = 512
    half = tn // 2
    nj = pl.cdiv(D_FF, tn)
    voff = D_FF // half
    last = 2 * D_FF // half - 1
    cw8 = jnp.repeat(conv_w, SUBLANES, axis=0)
    cb8 = jnp.broadcast_to(conv_b[None, :], (SUBLANES, 2 * D_FF))
    cwa, cwv = cw8[:, :D_FF], cw8[:, D_FF:]
    cba, cbv = cb8[:, :D_FF], cb8[:, D_FF:]
    hist2 = jnp.stack([hist[:, :, :D_FF], hist[:, :, D_FF:]], axis=0)
    if L >= 1024:
        tm = 1024
        tps = L // tm
        nti = T // tm

        def wspec(k):
            return pl.BlockSpec((None, D, half), lambda i, j: (l, 0, jnp.minimum(2 * j + k, last)))

        cwspec = pl.BlockSpec((CONV_W * SUBLANES, tn), lambda i, j: (0, j))
        cbspec = pl.BlockSpec((SUBLANES, tn), lambda i, j: (0, j))
        kern = functools.partial(_up_conv_long_kernel, tm=tm, tiles_per_seq=tps)
        g, tails = pl.pallas_call(
            kern,
            grid=(nti, nj),
            in_specs=[pl.BlockSpec((tm, D), lambda i, j: (i, 0)),
                      pl.BlockSpec((BF16_ROWS, D),
                                   lambda i, j: (jnp.maximum(i * (tm // BF16_ROWS) - 1, 0), 0)),
                      wspec(0), wspec(1), wspec(voff), wspec(voff + 1),
                      cwspec, cwspec, cbspec, cbspec,
                      pl.BlockSpec((2, 1, BF16_ROWS, tn), lambda i, j: (0, i // tps, 0, j))],
            out_specs=[pl.BlockSpec((tm, tn), lambda i, j: (i, j)),
                       pl.BlockSpec((2, 1, 2, tn), lambda i, j: (0, i, 0, j))],
            out_shape=[jax.ShapeDtypeStruct((T, D_FF), BF16),
                       jax.ShapeDtypeStruct((2, nti, 2, D_FF), F32)],
            scratch_shapes=[pltpu.VMEM((tm + BF16_ROWS, D), BF16)],
            compiler_params=_params("parallel", "arbitrary"),
            name="conv_ffn_up_long",
        )(xn, xn, w_up, w_up, w_up, w_up, cwa, cwv, cba, cbv,
          jnp.pad(hist2, ((0, 0), (0, 0), (BF16_ROWS - 2, 0), (0, 0))))
        tails = tails[:, tps - 1::tps]
    else:
        def wspec(k):
            return pl.BlockSpec((None, D, half), lambda j: (l, 0, jnp.minimum(2 * j + k, last)))

        kern = functools.partial(_up_conv_short_kernel, nseq=B, L=L)
        g, tails = pl.pallas_call(
            kern,
            grid=(nj,),
            in_specs=[pl.BlockSpec((T, D), lambda j: (0, 0)),
                      wspec(0), wspec(1), wspec(voff), wspec(voff + 1),
                      pl.BlockSpec((CONV_W * SUBLANES, tn), lambda j: (0, j)),
                      pl.BlockSpec((CONV_W * SUBLANES, tn), lambda j: (0, j)),
                      pl.BlockSpec((SUBLANES, tn), lambda j: (0, j)),
                      pl.BlockSpec((SUBLANES, tn), lambda j: (0, j)),
                      pl.BlockSpec((2, B, 2, tn), lambda j: (0, 0, 0, j))],
            out_specs=[pl.BlockSpec((T, tn), lambda j: (0, j)),
                       pl.BlockSpec((2, B, 2, tn), lambda j: (0, 0, 0, j))],
            out_shape=[jax.ShapeDtypeStruct((T, D_FF), BF16),
                       jax.ShapeDtypeStruct((2, B, 2, D_FF), F32)],
            scratch_shapes=[pltpu.VMEM((B, L + SUBLANES, tn), F32),
                            pltpu.VMEM((B, L + SUBLANES, tn), F32)],
            compiler_params=_params("arbitrary"),
            name="conv_ffn_up_short",
        )(xn, w_up, w_up, w_up, w_up, cwa, cwv, cba, cbv, hist2)
    s_conv = jnp.concatenate([tails[0], tails[1]], axis=-1)
    return g, s_conv


def _ple_kernel(x_ref, wg_ref, p_ref, wp_ref, h_ref, o_ref):
    x = x_ref[...]
    pb = p_ref[...].astype(BF16)
    for cols in _column_pieces(o_ref.shape[1]):
        gate = jax.nn.sigmoid(_dot(x, wg_ref[:, cols]))
        emb = _dot(pb, wp_ref[:, cols])
        o_ref[:, cols] = h_ref[:, cols] + gate * emb


def ple_update(xn, w_pg, p, w_ple, l, h, *, tm, tn):
    M, D = xn.shape
    return pl.pallas_call(
        _ple_kernel,
        grid=(M // tm, D // tn),
        in_specs=[pl.BlockSpec((tm, D), lambda i, j: (i, 0)),
                  pl.BlockSpec((None, D, tn), lambda i, j: (l, 0, j)),
                  pl.BlockSpec((tm, PLE_DIM), lambda i, j: (i, 0)),
                  pl.BlockSpec((None, PLE_DIM, tn), lambda i, j: (l, 0, j)),
                  pl.BlockSpec((tm, tn), lambda i, j: (i, j))],
        out_specs=pl.BlockSpec((tm, tn), lambda i, j: (i, j)),
        out_shape=jax.ShapeDtypeStruct((M, D), F32),
        input_output_aliases={4: 0},
        compiler_params=_params("parallel", "arbitrary"),
        name="ple_update",
    )(xn, w_pg, p, w_ple, h)


def _layer_stack(x, p, st_pool, st_hg, st_s5r, st_s5i, st_conv, pos0, lb, wts):
    B, L, D = x.shape
    T = B * L
    W = W_BRANCH
    tm = min(T, 1024)
    h = x.reshape(T, D)
    n_pool, n_hg, n_s5r, n_s5i, n_conv = [], [], [], [], []
    w = wts
    for l in range(DEPTH):
        xn = rms_norm(h, w['norm_mix'][l], BF16)
        proj = matmul(xn, w['w_in'], l, F32, tm=tm, tn=1024, col_block0=0, n_cols=N_MIX, name="in_proj_mix")
        gates = matmul(xn, w['w_in'], l, BF16, tm=tm, tn=1024, col_block0=N_MIX // 1024,
                       n_cols=N_BRANCH * D, name="in_proj_gates")
        o_a, s_pool = pool_mixer(proj, st_pool[l], w['pool_w'], l, w['pool_scale'][l], B=B, L=L, pos0=pos0)
        o_b, s_hg = hgrn2_mixer(proj, lb[l], w['hg_norm'][l], st_hg[l], B=B, L=L)
        y_c, s_r, s_i = s5_mixer(proj, st_s5r[l], st_s5i[l], w['s5_disc'][l], w['s5_d'][l], B=B, L=L)
        o_c = s5_glu(y_c, w['s5_w_glu'], l, w['s5_b_glu'][l], tm=tm, tn=1024)
        merged = branch_merge(o_a, o_b, o_c, w['w_br'], l, gates, tm=tm, tn=512)
        h = matmul_residual(merged, w['w_out'], l, h, tm=tm, tn=1024, name="out_proj", in_place=l > 0)
        xn = rms_norm(h, w['norm_ffn'][l], BF16)
        g, s_conv = conv_ffn_up(xn, w['w_up'], l, w['conv_w'][l], w['conv_b'][l], st_conv[l], B=B, L=L)
        h = matmul_residual(g, w['w_down'], l, h, tm=min(T, 512), tn=512, name="ffn_down")
        xn = rms_norm(h, w['norm_ple'][l], BF16)
        h = ple_update(xn, w['w_ple_gate'], p[l].reshape(T, PLE_DIM), w['w_ple'], l, h, tm=tm, tn=512)
        n_pool.append(s_pool)
        n_hg.append(s_hg)
        n_s5r.append(s_r)
        n_s5i.append(s_i)
        n_conv.append(s_conv)
    y = rms_norm(h, w['norm_final'], F32).reshape(B, L, D)
    return (y, jnp.stack(n_pool), jnp.stack(n_hg), jnp.stack(n_s5r), jnp.stack(n_s5i), jnp.stack(n_conv))


def kernel(x_prompt, x_sample, state_pool, state_hgrn, state_s5_re, state_s5_im, state_ffn_conv,
           p_prompt, p_sample, norm_mix, w_in, pool_w, pool_scale, hg_lb_logits, hg_norm,
           s5_a_re, s5_a_im, s5_log_step, s5_b_re, s5_b_im, s5_c_re, s5_c_im, s5_d, s5_w_glu, s5_b_glu,
           w_br, w_out, norm_ffn, w_up, conv_w, conv_b, w_down, norm_ple, w_ple_gate, w_ple, norm_final):
    lb = forget_lower_bounds(hg_lb_logits)
    wts = {
        'norm_mix': norm_mix, 'w_in': cast_bf16(w_in), 'pool_w': cast_bf16(pool_w),
        'pool_scale': pool_scale, 'hg_norm': hg_norm,
        's5_disc': [_s5_discretize(s5_a_re[l], s5_a_im[l], s5_log_step[l], s5_b_re[l], s5_b_im[l],
                                   s5_c_re[l], s5_c_im[l]) for l in range(DEPTH)],
        's5_d': s5_d, 's5_w_glu': cast_bf16(s5_w_glu), 's5_b_glu': s5_b_glu,
        'w_br': cast_bf16(w_br), 'w_out': cast_bf16(w_out), 'norm_ffn': norm_ffn,
        'w_up': cast_bf16(w_up), 'conv_w': conv_w, 'conv_b': conv_b, 'w_down': cast_bf16(w_down),
        'norm_ple': norm_ple, 'w_ple_gate': cast_bf16(w_ple_gate), 'w_ple': cast_bf16(w_ple),
        'norm_final': norm_final,
    }
    bp = x_prompt.shape[0]
    z_pool = jnp.zeros((DEPTH, bp) + state_pool.shape[2:], F32)
    z_hg = jnp.zeros((DEPTH, bp) + state_hgrn.shape[2:], F32)
    z_s5 = jnp.zeros((DEPTH, bp) + state_s5_re.shape[2:], F32)
    z_conv = jnp.zeros((DEPTH, bp) + state_ffn_conv.shape[2:], F32)
    y_p, pool_p, hg_p, s5r_p, s5i_p, conv_p = _layer_stack(
        x_prompt, p_prompt, z_pool, z_hg, z_s5, z_s5, z_conv, 0, lb, wts)
    y_s, pool_s, hg_s, s5r_s, s5i_s, conv_s = _layer_stack(
        x_sample, p_sample, state_pool, state_hgrn, state_s5_re, state_s5_im, state_ffn_conv,
        PAST_LEN, lb, wts)
    return (y_p, y_s, pool_p, hg_p, s5r_p, s5i_p, conv_p, pool_s, hg_s, s5r_s, s5i_s, conv_s)
```
